```python
import math
import jax
import jax.numpy as jnp
from jax import lax
import numpy as np

D_MODEL = 4096
BATCH = 4
SEQ = 2048
DEPTH = 1
DEC_BATCH = 128
DEC_SEQ = 8
PAST_LEN = 2048
PAGE_SIZE = 128

HEAD_DIM = 128
NSA_WIDTH = D_MODEL // 2
NSA_HEADS = NSA_WIDTH // HEAD_DIM
NSA_KV_HEADS = 4
NSA_GROUP = NSA_HEADS // NSA_KV_HEADS
N_BRANCH = 3
CMP_BLOCK = 32
CMP_STRIDE = 16
CMP_RATIO = CMP_BLOCK // CMP_STRIDE
SLC_BLOCK = 64
SLC_TOPN = 16
SLC_LOCAL = 2
SLC_QBLOCK = 128
WINDOW = 512
WIN_QBLOCK = 128
FORCE_SCORE = 1e4
ATTN_SCALE = HEAD_DIM ** -0.5
NEG_INF = -1e30

GM_WIDTH = D_MODEL - NSA_WIDTH
GM_HEAD_DIM = 128
GM_HEADS = GM_WIDTH // GM_HEAD_DIM
GM_CHUNK = 128

KV_COLS = N_BRANCH * 2 * NSA_KV_HEADS * HEAD_DIM
GATE_COLS = N_BRANCH * NSA_HEADS
IN_COLS = NSA_WIDTH + KV_COLS + GATE_COLS + 2 * GM_WIDTH

NUM_BUCKETS = 32
MAX_EXACT = NUM_BUCKETS // 2
MAX_DISTANCE = 128

N_EXPERTS = 32
TOP_K = 4
D_FF = D_MODEL
SWIGLU_LIMIT = 7.0
SWIGLU_ALPHA = 1.702
MOE_ROWS = 128

DN_ALPHA = (2 * DEPTH) ** 0.25
DN_BETA = (8 * DEPTH) ** -0.25
LN_EPS = 1e-5

kernel_name = 'nsa_gmlp_moe_hybrid_step'


def _normalize(x):
    xf = x.astype(jnp.float32)
    xc = xf - jnp.mean(xf, axis=-1, keepdims=True)
    return xc * lax.rsqrt(jnp.mean(xc * xc, axis=-1, keepdims=True) + LN_EPS)


def layer_norm(x, g, b):
    return (_normalize(x) * g + b).astype(x.dtype)


def modulate(x, shift, scale):
    return (_normalize(x) * (1.0 + scale[:, None, :]) + shift[:, None, :]).astype(x.dtype)


def rms_norm(x, g):
    xf = x.astype(jnp.float32)
    return (xf * lax.rsqrt(jnp.mean(xf * xf, axis=-1, keepdims=True) + LN_EPS) * g).astype(x.dtype)


def ada_modulation(c, w, b):
    return jnp.split(jax.nn.silu(c) @ w + b, 6, axis=-1)


def t5_bucket(dist):
    n = jnp.maximum(dist, 0)
    nf = jnp.maximum(n, 1).astype(jnp.float32)
    large = MAX_EXACT + (jnp.log(nf / MAX_EXACT) / math.log(MAX_DISTANCE / MAX_EXACT)
                         * (NUM_BUCKETS - MAX_EXACT)).astype(jnp.int32)
    return jnp.where(n < MAX_EXACT, n, jnp.minimum(large, NUM_BUCKETS - 1))


def rel_bias(table, dist):
    return table[t5_bucket(dist)].reshape(*dist.shape, NSA_KV_HEADS, NSA_GROUP)


def group_rel_bias(table, dist):
    tb = table.reshape(NUM_BUCKETS, NSA_KV_HEADS, NSA_GROUP).transpose(1, 0, 2)
    return jax.vmap(lambda t, bk: t[bk], in_axes=(0, 1), out_axes=1)(tb, t5_bucket(dist))


def attend_keys(q, k, v, dist, valid, table):
    bias = jnp.moveaxis(rel_bias(table, dist), (-2, -1), (0, 1))
    s = jnp.einsum('btgjd,bsgd->bgjts', q, k).astype(jnp.float32) * ATTN_SCALE + bias
    s = jnp.where(valid, s, NEG_INF)
    p = jax.nn.softmax(s, axis=-1) * valid
    o = jnp.einsum('bgjts,bsgd->btgjd', p.astype(v.dtype), v)
    return o, p


def window_attention_banded(q, k, v, table):
    B, T = q.shape[:2]
    nb = T // WIN_QBLOCK
    span = WINDOW + WIN_QBLOCK
    pad = ((0, 0), (WINDOW, 0), (0, 0), (0, 0))
    rows = jnp.arange(nb)[:, None] * WIN_QBLOCK + jnp.arange(span)[None, :]
    kb = jnp.pad(k, pad)[:, rows]
    vb = jnp.pad(v, pad)[:, rows]
    qb = q.reshape(B, nb, WIN_QBLOCK, NSA_KV_HEADS, NSA_GROUP, HEAD_DIM)
    dist = jnp.arange(WIN_QBLOCK)[:, None] + WINDOW - jnp.arange(span)[None, :]
    valid = (dist >= 0) & (dist < WINDOW) & (rows[:, None, :] >= WINDOW)
    bias = jnp.moveaxis(rel_bias(table, dist), (-2, -1), (0, 1))
    s = jnp.einsum('bnqgjd,bnkgd->bngjqk', qb, kb).astype(jnp.float32) * ATTN_SCALE + bias
    s = jnp.where(valid[None, :, None, None], s, NEG_INF)
    p = jax.nn.softmax(s, axis=-1)
    o = jnp.einsum('bngjqk,bnkgd->bnqgjd', p.astype(vb.dtype), vb)
    return o.reshape(B, T, NSA_KV_HEADS, NSA_GROUP, HEAD_DIM)


def compress_blocks(rows, pe, w1, b1, w2, b2):
    B, L = rows.shape[:2]
    n_chunks = L // CMP_STRIDE
    n_cmp = n_chunks - CMP_RATIO + 1
    ch = rows[:, :n_chunks * CMP_STRIDE].reshape(B, n_chunks, CMP_STRIDE, 2, NSA_KV_HEADS, HEAD_DIM)
    pe_r = pe.reshape(2, CMP_RATIO, CMP_STRIDE, HEAD_DIM)
    w1_r = w1.reshape(2, CMP_RATIO, CMP_STRIDE, HEAD_DIM, HEAD_DIM)
    h = b1[None, None, :, None, :]
    for r in range(CMP_RATIO):
        seg = ch[:, r:r + n_cmp] + jnp.transpose(pe_r[:, r], (1, 0, 2))[:, :, None, :]
        h = h + jnp.einsum('bcjsgd,sjde->bcsge', seg, w1_r[:, r])
    return jnp.einsum('bcsgd,sde->bcsge', jax.nn.gelu(h), w2) + b2[None, None, :, None, :]


def select_blocks(p_cmp, qpos, n_slc):
    nc = p_cmp.shape[-1]
    c0 = jnp.arange(nc) * CMP_STRIDE
    s0 = jnp.arange(n_slc) * SLC_BLOCK
    cover = ((c0[:, None] < s0[None, :] + SLC_BLOCK)
             & (c0[:, None] + CMP_BLOCK > s0[None, :])).astype(p_cmp.dtype)
    imp = jnp.einsum('bgjtc,cs->bgts', p_cmp, cover)
    cur = (qpos // SLC_BLOCK)[:, None]
    blk = jnp.arange(n_slc)[None, :]
    forced = (blk == 0) | ((cur - blk >= 0) & (cur - blk < SLC_LOCAL))
    score = jnp.where(blk <= cur, imp + forced * FORCE_SCORE, -1.0)
    _, idx = lax.top_k(score, min(SLC_TOPN, n_slc))
    return idx


def selected_attention_block(q, qpos, idx, kb, vb, table):
    take = jax.vmap(jax.vmap(lambda blocks, ix: blocks[ix]))
    kg = take(kb, idx)
    vg = take(vb, idx)
    kpos = idx[..., None] * SLC_BLOCK + jnp.arange(SLC_BLOCK)
    dist = qpos[:, None, None] - kpos
    bias = jnp.moveaxis(group_rel_bias(table, dist), -1, 3)
    s = jnp.einsum('btgjd,bgtnkd->bgtjnk', q, kg).astype(jnp.float32) * ATTN_SCALE + bias
    s = jnp.where((dist >= 0)[:, :, :, None], s, NEG_INF)
    shp = s.shape
    p = jax.nn.softmax(s.reshape(*shp[:4], -1), axis=-1).reshape(shp)
    return jnp.einsum('bgtjnk,bgtnkd->btgjd', p.astype(vg.dtype), vg)


def map_query_blocks(fn, q_block, q, qpos, idx):
    B, T = q.shape[:2]
    nb = T // q_block
    qs = jnp.moveaxis(q.reshape(B, nb, q_block, *q.shape[2:]), 1, 0)
    ps = qpos.reshape(nb, q_block)
    ids = jnp.moveaxis(idx.reshape(*idx.shape[:2], nb, q_block, idx.shape[-1]), 2, 0)
    out = lax.map(lambda a: fn(*a), (qs, ps, ids))
    return jnp.moveaxis(out, 0, 1).reshape(B, T, *out.shape[3:])


def nsa_compressed_selected(q, qpos, cmp_rows, slc_rows, lp, table, q_block):
    B, L = slc_rows.shape[:2]
    ckv = compress_blocks(cmp_rows, lp['cmp_pe'], lp['cmp_w1'], lp['cmp_b1'], lp['cmp_w2'], lp['cmp_b2'])
    n_cmp = ckv.shape[1]
    dist = qpos[:, None] - (jnp.arange(n_cmp) * CMP_STRIDE + CMP_BLOCK - 1)[None, :]
    o_cmp, p_cmp = attend_keys(q, ckv[:, :, 0], ckv[:, :, 1], dist, dist >= 0, table)
    n_slc = -(-L // SLC_BLOCK)
    idx = select_blocks(p_cmp, qpos, n_slc)
    rows = jnp.pad(slc_rows, ((0, 0), (0, n_slc * SLC_BLOCK - L), (0, 0), (0, 0), (0, 0)))
    blocks = rows.reshape(B, n_slc, SLC_BLOCK, 2, NSA_KV_HEADS, HEAD_DIM).transpose(3, 0, 4, 1, 2, 5)
    kb, vb = blocks[0], blocks[1]
    o_slc = map_query_blocks(lambda qq, pp, ii: selected_attention_block(qq, pp, ii, kb, vb, table),
                             q_block, q, qpos, idx)
    return o_cmp, o_slc


def gmlp_mix(uv, lp):
    B, T = uv.shape[:2]
    u, v = jnp.split(uv, 2, axis=-1)
    u = u.reshape(B, T, GM_HEADS, GM_HEAD_DIM)
    v = layer_norm(v.reshape(B, T, GM_HEADS, GM_HEAD_DIM), lp['gm_ln_g'], lp['gm_ln_b'])
    cl = min(T, GM_CHUNK)
    w_s = jnp.where(jnp.tril(jnp.ones((cl, cl), bool)), lp['gm_ws'][:, :cl, :cl], 0)
    s = jnp.einsum('hij,bcjhd->bcihd', w_s, v.reshape(B, T // cl, cl, GM_HEADS, GM_HEAD_DIM))
    s = s + jnp.transpose(lp['gm_bs'][:, :cl])[None, None, :, :, None]
    return (u * s.reshape(B, T, GM_HEADS, GM_HEAD_DIM)).reshape(B, T, GM_WIDTH), v


def project_mixer_input(m, w_in):
    B, T, _ = m.shape
    p = m @ w_in
    q, kv, g, uv = jnp.split(p, [NSA_WIDTH, NSA_WIDTH + KV_COLS, NSA_WIDTH + KV_COLS + GATE_COLS], axis=-1)
    q = q.reshape(B, T, NSA_KV_HEADS, NSA_GROUP, HEAD_DIM)
    kv = kv.reshape(B, T, 2 * N_BRANCH, NSA_KV_HEADS, HEAD_DIM)
    g = jax.nn.sigmoid(g.astype(jnp.float32)).astype(m.dtype)
    g = g.reshape(B, T, N_BRANCH, NSA_KV_HEADS, NSA_GROUP, 1)
    return q, kv, g, jax.nn.gelu(uv)


def merge_heads(o_cmp, o_slc, o_win, g, gm_out, lp):
    B, T = o_cmp.shape[:2]
    o = g[:, :, 0] * o_cmp + g[:, :, 1] * o_slc + g[:, :, 2] * o_win
    o = rms_norm(o.reshape(B, T, NSA_WIDTH), lp['g_nsa'])
    gm = rms_norm(gm_out, lp['g_gm'])
    return jnp.concatenate([o, gm], axis=-1) @ lp['w_out']


def mixer_prompt(m, lp, table):
    B, T, _ = m.shape
    q, kv, g, uv = project_mixer_input(m, lp['w_in'])
    qpos = jnp.arange(T)
    o_cmp, o_slc = nsa_compressed_selected(q, qpos, kv[:, :, 0:2], kv[:, :, 2:4], lp, table, SLC_QBLOCK)
    o_win = window_attention_banded(q, kv[:, :, 4], kv[:, :, 5], table)
    gm_out, _ = gmlp_mix(uv, lp)
    y = merge_heads(o_cmp, o_slc, o_win, g, gm_out, lp)
    wb = min(WINDOW, T)
    return y, kv[:, :, 0:4], kv[:, T - wb:, 4:6]


def mixer_sample(m, cache_kv_pages, state_kv_window, page_table, layer, lp, table):
    B, T, _ = m.shape
    q, kv, g, uv = project_mixer_input(m, lp['w_in'])
    past_len = page_table.shape[1] * cache_kv_pages.shape[2]
    past = cache_kv_pages[layer, page_table].reshape(B, past_len, *cache_kv_pages.shape[3:])
    full = jnp.concatenate([past, kv[:, :, 0:4]], axis=1)
    qpos = past_len + jnp.arange(T)
    o_cmp, o_slc = nsa_compressed_selected(q, qpos, full[:, :, 0:2], full[:, :, 2:4], lp, table, 1)
    win_state = state_kv_window[layer]
    wbuf = win_state.shape[1]
    win_all = jnp.concatenate([win_state, kv[:, :, 4:6]], axis=1)
    kpos = past_len - wbuf + jnp.arange(wbuf + T)
    dist = qpos[:, None] - kpos[None, :]
    o_win, _ = attend_keys(q, win_all[:, :, 0], win_all[:, :, 1], dist, (dist >= 0) & (dist < WINDOW), table)
    gm_out, gm_v = gmlp_mix(uv, lp)
    y = merge_heads(o_cmp, o_slc, o_win, g, gm_out, lp)
    return y, kv[:, :, 0:4], win_all[:, T:], gm_v


def moe_ffn(h, layer, w_router, b_router, w_gu, b_gu, w_down, b_down):
    N, D = h.shape
    logits = (h @ w_router[layer]).astype(jnp.float32) + b_router[layer]
    top_val, top_idx = lax.top_k(logits, TOP_K)
    gates = jax.nn.softmax(top_val, axis=-1)
    n_assign = N * TOP_K
    n_blocks = -(-(n_assign + N_EXPERTS * (MOE_ROWS - 1)) // MOE_ROWS)
    flat_e = top_idx.reshape(-1)
    order = jnp.argsort(flat_e)
    e_sorted = flat_e[order]
    counts = jnp.bincount(flat_e, length=N_EXPERTS)
    padded = (counts + MOE_ROWS - 1) // MOE_ROWS * MOE_ROWS
    pad_end = jnp.cumsum(padded)
    pad_start = pad_end - padded
    start = jnp.cumsum(counts) - counts
    dest = pad_start[e_sorted] + jnp.arange(n_assign) - start[e_sorted]
    n_rows = n_blocks * MOE_ROWS
    row_tok = jnp.zeros((n_rows,), jnp.int32).at[dest].set((order // TOP_K).astype(jnp.int32))
    row_gate = jnp.zeros((n_rows,), jnp.float32).at[dest].set(gates.reshape(-1)[order])
    blk_e = jnp.minimum(jnp.searchsorted(pad_end, jnp.arange(n_blocks) * MOE_ROWS, side='right'), N_EXPERTS - 1)
    xb = h[row_tok].reshape(n_blocks, MOE_ROWS, D)

    def expert_block(args):
        xr, e = args
        gu = xr @ w_gu[layer, e] + b_gu[layer, e]
        gl, lin = jnp.split(gu, 2, axis=-1)
        gl = jnp.minimum(gl, SWIGLU_LIMIT)
        lin = jnp.clip(lin, -SWIGLU_LIMIT, SWIGLU_LIMIT)
        return ((lin + 1.0) * (gl * jax.nn.sigmoid(SWIGLU_ALPHA * gl))) @ w_down[layer, e] + b_down[layer, e]

    yb = lax.map(expert_block, (xb, blk_e)).reshape(n_rows, D)
    return jnp.zeros_like(h).at[row_tok].add(yb * row_gate[:, None].astype(yb.dtype))


def setup_inputs(seed: int = 0) -> dict:
    key = jax.random.key(seed)
    keys = iter(jax.random.split(key, 48))

    def nrm(shape, scale):
        return jax.random.normal(next(keys), shape, jnp.float32) * scale

    def gain(shape):
        return 1.0 + nrm(shape, 0.05)

    n_pages = PAST_LEN // PAGE_SIZE
    n_pool = (5 * DEC_BATCH * n_pages) // 4
    wbuf = min(WINDOW, PAST_LEN)
    page_table = jax.random.permutation(next(keys), n_pool)[:DEC_BATCH * n_pages]
    page_table = page_table.reshape(DEC_BATCH, n_pages).astype(jnp.int32)
    d_in = D_MODEL ** -0.5
    return {
        'x_prompt': nrm((BATCH, SEQ, D_MODEL), 1.0),
        'x_sample': nrm((DEC_BATCH, DEC_SEQ, D_MODEL), 1.0),
        'cache_kv_pages': nrm((DEPTH, n_pool, PAGE_SIZE, 4, NSA_KV_HEADS, HEAD_DIM), 1.0),
        'state_kv_window': nrm((DEPTH, DEC_BATCH, wbuf, 2, NSA_KV_HEADS, HEAD_DIM), 1.0),
        'page_table': page_table,
        'c_prompt': nrm((BATCH, D_MODEL), 1.0),
        'c_sample': nrm((DEC_BATCH, D_MODEL), 1.0),
        'rel_bias_table': nrm((NUM_BUCKETS, NSA_HEADS), 0.3),
        'w_ada': nrm((DEPTH, D_MODEL, 6 * D_MODEL), 0.5 * d_in),
        'b_ada': nrm((DEPTH, 6 * D_MODEL), 0.01),
        'w_in': nrm((DEPTH, D_MODEL, IN_COLS), d_in),
        'cmp_pe': nrm((DEPTH, 2, CMP_BLOCK, HEAD_DIM), 0.3),
        'cmp_w1': nrm((DEPTH, 2, CMP_BLOCK * HEAD_DIM, HEAD_DIM), (CMP_BLOCK * HEAD_DIM) ** -0.5),
        'cmp_b1': nrm((DEPTH, 2, HEAD_DIM), 0.01),
        'cmp_w2': nrm((DEPTH, 2, HEAD_DIM, HEAD_DIM), HEAD_DIM ** -0.5),
        'cmp_b2': nrm((DEPTH, 2, HEAD_DIM), 0.01),
        'gm_ln_g': gain((DEPTH, GM_HEADS, GM_HEAD_DIM)),
        'gm_ln_b': nrm((DEPTH, GM_HEADS, GM_HEAD_DIM), 0.01),
        'gm_ws': nrm((DEPTH, GM_HEADS, GM_CHUNK, GM_CHUNK), GM_CHUNK ** -0.5),
        'gm_bs': gain((DEPTH, GM_HEADS, GM_CHUNK)),
        'g_norm_nsa': gain((DEPTH, NSA_WIDTH)),
        'g_norm_gm': gain((DEPTH, GM_WIDTH)),
        'w_out': nrm((DEPTH, D_MODEL, D_MODEL), DN_BETA * d_in),
        'ln1_g': gain((DEPTH, D_MODEL)),
        'ln1_b': nrm((DEPTH, D_MODEL), 0.01),
        'w_router': nrm((DEPTH, D_MODEL, N_EXPERTS), d_in),
        'b_router': nrm((DEPTH, N_EXPERTS), 0.01),
        'w_gu': nrm((DEPTH, N_EXPERTS, D_MODEL, 2 * D_FF), d_in),
        'b_gu': nrm((DEPTH, N_EXPERTS, 2 * D_FF), 0.01),
        'w_down': nrm((DEPTH, N_EXPERTS, D_FF, D_MODEL), DN_BETA * D_FF ** -0.5),
        'b_down': nrm((DEPTH, N_EXPERTS, D_MODEL), 0.01),
        'ln2_g': gain((DEPTH, D_MODEL)),
        'ln2_b': nrm((DEPTH, D_MODEL), 0.01),
    }


def reference(x_prompt, x_sample, cache_kv_pages, state_kv_window, page_table, c_prompt, c_sample,
              rel_bias_table, w_ada, b_ada, w_in, cmp_pe, cmp_w1, cmp_b1, cmp_w2, cmp_b2,
              gm_ln_g, gm_ln_b, gm_ws, gm_bs, g_norm_nsa, g_norm_gm, w_out, ln1_g, ln1_b,
              w_router, b_router, w_gu, b_gu, w_down, b_down, ln2_g, ln2_b):
    y_p, y_s = x_prompt, x_sample
    p_rows, p_win, s_rows, s_win, s_gmv = [], [], [], [], []
    for l in range(DEPTH):
        lp = {'w_in': w_in[l], 'cmp_pe': cmp_pe[l], 'cmp_w1': cmp_w1[l], 'cmp_b1': cmp_b1[l],
              'cmp_w2': cmp_w2[l], 'cmp_b2': cmp_b2[l], 'gm_ln_g': gm_ln_g[l], 'gm_ln_b': gm_ln_b[l],
              'gm_ws': gm_ws[l], 'gm_bs': gm_bs[l], 'g_nsa': g_norm_nsa[l], 'g_gm': g_norm_gm[l],
              'w_out': w_out[l]}
        sh_mp, sc_mp, gt_mp, sh_fp, sc_fp, gt_fp = ada_modulation(c_prompt, w_ada[l], b_ada[l])
        sh_ms, sc_ms, gt_ms, sh_fs, sc_fs, gt_fs = ada_modulation(c_sample, w_ada[l], b_ada[l])
        mix_p, kv_p, win_p = mixer_prompt(modulate(y_p, sh_mp, sc_mp), lp, rel_bias_table)
        mix_s, kv_s, win_s, gmv_s = mixer_sample(modulate(y_s, sh_ms, sc_ms), cache_kv_pages,
                                                 state_kv_window, page_table, l, lp, rel_bias_table)
        y_p = layer_norm(DN_ALPHA * y_p + gt_mp[:, None, :] * mix_p, ln1_g[l], ln1_b[l])
        y_s = layer_norm(DN_ALPHA * y_s + gt_ms[:, None, :] * mix_s, ln1_g[l], ln1_b[l])
        h_p = modulate(y_p, sh_fp, sc_fp).reshape(-1, D_MODEL)
        h_s = modulate(y_s, sh_fs, sc_fs).reshape(-1, D_MODEL)
        f = moe_ffn(jnp.concatenate([h_p, h_s], axis=0), l, w_router, b_router, w_gu, b_gu, w_down, b_down)
        n_p = h_p.shape[0]
        y_p = layer_norm(DN_ALPHA * y_p + gt_fp[:, None, :] * f[:n_p].reshape(y_p.shape), ln2_g[l], ln2_b[l])
        y_s = layer_norm(DN_ALPHA * y_s + gt_fs[:, None, :] * f[n_p:].reshape(y_s.shape), ln2_g[l], ln2_b[l])
        p_rows.append(kv_p)
        p_win.append(win_p)
        s_rows.append(kv_s)
        s_win.append(win_s)
        s_gmv.append(gmv_s)
    prompt_kv_rows = jnp.stack(p_rows)
    prompt_kv_window = jnp.stack(p_win)
    sample_kv_rows = jnp.stack(s_rows)
    sample_kv_window = jnp.stack(s_win)
    sample_gmlp_v = jnp.stack(s_gmv)
    return (y_p, y_s, prompt_kv_rows, prompt_kv_window, sample_kv_rows, sample_kv_window, sample_gmlp_v)
```

```python
import functools
import math

import numpy as np
import jax
import jax.numpy as jnp
from jax import lax
from jax.experimental import pallas as pl
from jax.experimental.pallas import tpu as pltpu

F32 = jnp.float32
BF16 = jnp.bfloat16
SDS = jax.ShapeDtypeStruct

D_MODEL = 4096
BATCH = 4
SEQ = 2048
DEC_BATCH = 128
DEC_SEQ = 8
PAST_LEN = 2048
PAGE_SIZE = 128
N_PAGES = PAST_LEN // PAGE_SIZE
HEAD_DIM = 128
NSA_WIDTH = D_MODEL // 2
NSA_HEADS = NSA_WIDTH // HEAD_DIM
NSA_KV_HEADS = 4
NSA_GROUP = NSA_HEADS // NSA_KV_HEADS
N_BRANCH = 3
CMP_BLOCK = 32
CMP_STRIDE = 16
SLC_BLOCK = 64
SLC_TOPN = 16
SLC_LOCAL = 2
WINDOW = 512
FORCE_SCORE = 1e4
ATTN_SCALE = HEAD_DIM ** -0.5
NEG_INF = -1e30
GM_WIDTH = D_MODEL - NSA_WIDTH
GM_HEAD_DIM = 128
GM_HEADS = GM_WIDTH // GM_HEAD_DIM
GM_CHUNK = 128
KV_COLS = N_BRANCH * 2 * NSA_KV_HEADS * HEAD_DIM
GATE_COLS = N_BRANCH * NSA_HEADS
NUM_BUCKETS = 32
MAX_EXACT = NUM_BUCKETS // 2
MAX_DISTANCE = 128
N_EXPERTS = 32
TOP_K = 4
D_FF = D_MODEL
SWIGLU_LIMIT = 7.0
SWIGLU_ALPHA = 1.702
DN_ALPHA = 2.0 ** 0.25
LN_EPS = 1e-5

N_PROMPT = BATCH * SEQ
N_SAMPLE = DEC_BATCH * DEC_SEQ
N_TOK = N_PROMPT + N_SAMPLE

COL_Q = 0
COL_KV = NSA_WIDTH
COL_UV = NSA_WIDTH + KV_COLS
COL_G = COL_UV + 2 * GM_WIDTH
IN_COLS_PAD = 9728

VMEM_LIMIT = 56 * 1024 * 1024


def _cparams(sem):
    return pltpu.CompilerParams(dimension_semantics=sem, vmem_limit_bytes=VMEM_LIMIT)


def _normalize(x):
    xc = x - jnp.mean(x, axis=-1, keepdims=True)
    return xc * lax.rsqrt(jnp.mean(xc * xc, axis=-1, keepdims=True) + LN_EPS)


def _ada_body(c_ref, w_ref, b_ref, o_ref):
    c = c_ref[...]
    a = (c * jax.nn.sigmoid(c)).astype(BF16)
    o_ref[...] = jnp.dot(a, w_ref[...].astype(BF16), preferred_element_type=F32) + b_ref[...]


def _ada(c_all, w_ada, b_ada):
    rows = c_all.shape[0]
    tn = 512
    return pl.pallas_call(
        _ada_body,
        out_shape=SDS((rows, 6 * D_MODEL), F32),
        grid=(6 * D_MODEL // tn,),
        in_specs=[pl.BlockSpec((rows, D_MODEL), lambda j: (0, 0)),
                  pl.BlockSpec((D_MODEL, tn), lambda j: (0, j)),
                  pl.BlockSpec((1, tn), lambda j: (0, j))],
        out_specs=pl.BlockSpec((rows, tn), lambda j: (0, j)),
        compiler_params=_cparams(("arbitrary",)),
        name="ada",
    )(c_all, w_ada, b_ada.reshape(1, -1))


def _inproj_body(x_ref, sc_ref, sh_ref, w_ref, o_ref, m_ref):
    @pl.when(pl.program_id(1) == 0)
    def _():
        rows = x_ref.shape[0]
        per_token = sc_ref.shape[1] != 1
        for r0 in range(0, rows, 128):
            rs = slice(r0, r0 + 128)
            sc = sc_ref[0, rs] if per_token else sc_ref[0]
            sh = sh_ref[0, rs] if per_token else sh_ref[0]
            m_ref[rs] = (_normalize(x_ref[rs]) * (1.0 + sc) + sh).astype(BF16)

    o_ref[...] = jnp.dot(m_ref[...], w_ref[...], preferred_element_type=F32)


def _mod_spec(tile_rows, rows_per_group):
    if rows_per_group is None:
        return pl.BlockSpec((1, tile_rows, D_MODEL), lambda i, *_: (i, 0, 0))
    per = rows_per_group // tile_rows
    return pl.BlockSpec((1, 1, D_MODEL), lambda i, *_: (i // per, 0, 0))


def _inproj(x, sc, sh, w_bf16, rows_per_group, tm):
    n = x.shape[0]
    tn = 512
    ncols = w_bf16.shape[1]
    return pl.pallas_call(
        _inproj_body,
        out_shape=SDS((n, ncols), F32),
        grid=(n // tm, ncols // tn),
        in_specs=[pl.BlockSpec((tm, D_MODEL), lambda i, j: (i, 0)),
                  _mod_spec(tm, rows_per_group), _mod_spec(tm, rows_per_group),
                  pl.BlockSpec((D_MODEL, tn), lambda i, j: (0, j))],
        out_specs=pl.BlockSpec((tm, tn), lambda i, j: (i, j)),
        scratch_shapes=[pltpu.VMEM((tm, D_MODEL), BF16)],
        compiler_params=_cparams(("arbitrary", "arbitrary")),
        name="inproj",
    )(x, sc, sh, w_bf16)


def _merge_body(o_ref, gm_ref, gn_ref, gg_ref, cat_ref):
    o = o_ref[...]
    gm = gm_ref[...]
    on = o * lax.rsqrt(jnp.mean(o * o, axis=-1, keepdims=True) + LN_EPS) * gn_ref[...]
    gn = gm * lax.rsqrt(jnp.mean(gm * gm, axis=-1, keepdims=True) + LN_EPS) * gg_ref[...]
    cat_ref[:, :NSA_WIDTH] = on.astype(BF16)
    cat_ref[:, NSA_WIDTH:] = gn.astype(BF16)


def _merge(o_nsa, gm_out, g_nsa, g_gm):
    n = o_nsa.shape[0]
    tr = 512
    return pl.pallas_call(
        _merge_body,
        out_shape=SDS((n, D_MODEL), BF16),
        grid=(n // tr,),
        in_specs=[pl.BlockSpec((tr, NSA_WIDTH), lambda i: (i, 0)),
                  pl.BlockSpec((tr, GM_WIDTH), lambda i: (i, 0)),
                  pl.BlockSpec((1, NSA_WIDTH), lambda i: (0, 0)),
                  pl.BlockSpec((1, GM_WIDTH), lambda i: (0, 0))],
        out_specs=pl.BlockSpec((tr, D_MODEL), lambda i: (i, 0)),
        compiler_params=_cparams(("arbitrary",)),
        name="merge",
    )(o_nsa, gm_out, g_nsa.reshape(1, -1), g_gm.reshape(1, -1))


def _mm_body(x_ref, w_ref, o_ref):
    o_ref[...] = jnp.dot(x_ref[...], w_ref[...], preferred_element_type=F32)


def _matmul(x_bf16, w_bf16, tm=512, tn=1024):
    n, k = x_bf16.shape
    ncols = w_bf16.shape[1]
    return pl.pallas_call(
        _mm_body,
        out_shape=SDS((n, ncols), F32),
        grid=(n // tm, ncols // tn),
        in_specs=[pl.BlockSpec((tm, k), lambda i, j: (i, 0)),
                  pl.BlockSpec((k, tn), lambda i, j: (0, j))],
        out_specs=pl.BlockSpec((tm, tn), lambda i, j: (i, j)),
        compiler_params=_cparams(("arbitrary", "arbitrary")),
        name="outproj",
    )(x_bf16, w_bf16)


POST_T = 128


def _post1_body(x_ref, mix_ref, gt_ref, sc_ref, sh_ref, g_ref, b_ref, wr_ref, br_ref,
                y_ref, h_ref, lg_ref):
    z = DN_ALPHA * x_ref[...] + gt_ref[0] * mix_ref[...]
    y = _normalize(z) * g_ref[...] + b_ref[...]
    y_ref[...] = y
    h = _normalize(y) * (1.0 + sc_ref[0]) + sh_ref[0]
    h_ref[...] = h
    lg_ref[...] = jnp.dot(h.astype(BF16), wr_ref[...], preferred_element_type=F32) + br_ref[...]


def _post1(x, mix, gt, sc, sh, ln_g, ln_b, wr_bf16, br, rows_per_group):
    n = x.shape[0]
    tr = POST_T
    row = pl.BlockSpec((tr, D_MODEL), lambda i: (i, 0))
    vec = pl.BlockSpec((1, D_MODEL), lambda i: (0, 0))
    ms = _mod_spec(tr, rows_per_group)
    return pl.pallas_call(
        _post1_body,
        out_shape=(SDS((n, D_MODEL), F32), SDS((n, D_MODEL), F32), SDS((n, 128), F32)),
        grid=(n // tr,),
        in_specs=[row, row, ms, ms, ms, vec, vec,
                  pl.BlockSpec((D_MODEL, 128), lambda i: (0, 0)),
                  pl.BlockSpec((1, 128), lambda i: (0, 0))],
        out_specs=(row, row, pl.BlockSpec((tr, 128), lambda i: (i, 0))),
        compiler_params=_cparams(("arbitrary",)),
        name="post1",
    )(x, mix, gt, sc, sh, ln_g.reshape(1, -1), ln_b.reshape(1, -1), wr_bf16, br)


MOE_TM = 256
MOE_TILES = -(-(N_TOK * TOP_K + N_EXPERTS * (MOE_TM - 1)) // MOE_TM)
MOE_ROWS_PAD = MOE_TILES * MOE_TM


def _gather_body(tok_ref, nu_ref, h_ref, o_ref, buf_ref, sem):
    i = pl.program_id(0)

    @pl.when(i < nu_ref[0])
    def _():
        base = i * MOE_TM

        def row_copy(r, tok):
            return pltpu.make_async_copy(h_ref.at[pl.ds(tok, 1), :], buf_ref.at[pl.ds(r, 1), :], sem)

        def issue(r, c):
            row_copy(r, tok_ref[base + r]).start()
            return c

        lax.fori_loop(0, MOE_TM, issue, 0)

        def drain(r, c):
            row_copy(r, 0).wait()
            return c

        lax.fori_loop(0, MOE_TM, drain, 0)
        o_ref[...] = buf_ref[...].astype(BF16)

    @pl.when(i >= nu_ref[0])
    def _():
        o_ref[...] = jnp.zeros_like(o_ref)


def _moe_gather(row_tok, n_used, h):
    return pl.pallas_call(
        _gather_body,
        out_shape=SDS((MOE_ROWS_PAD, D_MODEL), BF16),
        grid_spec=pltpu.PrefetchScalarGridSpec(
            num_scalar_prefetch=2,
            grid=(MOE_TILES,),
            in_specs=[pl.BlockSpec(memory_space=pl.ANY)],
            out_specs=pl.BlockSpec((MOE_TM, D_MODEL), lambda i, tok, nu: (i, 0)),
            scratch_shapes=[pltpu.VMEM((MOE_TM, D_MODEL), F32), pltpu.SemaphoreType.DMA(())]),
        compiler_params=_cparams(("arbitrary",)),
        name="moe_gather",
    )(row_tok, n_used, h)


def _first_of_expert(m, te_ref):
    return (m == 0) | (te_ref[m] != te_ref[jnp.maximum(m - 1, 0)])


def _gu_body(te_ref, nu_ref, x_ref, wg_ref, wl_ref, bg_ref, bl_ref, o_ref, wgb_ref, wlb_ref):
    m = pl.program_id(1)

    @pl.when(m < nu_ref[0])
    def _():
        @pl.when(_first_of_expert(m, te_ref))
        def _():
            wgb_ref[...] = wg_ref[...].astype(BF16)
            wlb_ref[...] = wl_ref[...].astype(BF16)

        x = x_ref[...]
        gl = jnp.dot(x, wgb_ref[...], preferred_element_type=F32) + bg_ref[...]
        lin = jnp.dot(x, wlb_ref[...], preferred_element_type=F32) + bl_ref[...]
        gl = jnp.minimum(gl, SWIGLU_LIMIT)
        lin = jnp.clip(lin, -SWIGLU_LIMIT, SWIGLU_LIMIT)
        o_ref[...] = ((lin + 1.0) * (gl * jax.nn.sigmoid(SWIGLU_ALPHA * gl))).astype(BF16)

    @pl.when(m >= nu_ref[0])
    def _():
        o_ref[...] = jnp.zeros_like(o_ref)


def _moe_gate_up(tile_e, n_used, xs, w_gu, b_gu):
    tn = 512
    nt = D_FF // tn

    def mm(m, nu):
        return jnp.minimum(m, nu[0] - 1)

    return pl.pallas_call(
        _gu_body,
        out_shape=SDS((MOE_ROWS_PAD, D_FF), BF16),
        grid_spec=pltpu.PrefetchScalarGridSpec(
            num_scalar_prefetch=2,
            grid=(nt, MOE_TILES),
            in_specs=[pl.BlockSpec((MOE_TM, D_MODEL), lambda n, m, te, nu: (mm(m, nu), 0)),
                      pl.BlockSpec((None, D_MODEL, tn), lambda n, m, te, nu: (te[mm(m, nu)], 0, n)),
                      pl.BlockSpec((None, D_MODEL, tn), lambda n, m, te, nu: (te[mm(m, nu)], 0, n + nt)),
                      pl.BlockSpec((None, 1, tn), lambda n, m, te, nu: (te[mm(m, nu)], 0, n)),
                      pl.BlockSpec((None, 1, tn), lambda n, m, te, nu: (te[mm(m, nu)], 0, n + nt))],
            out_specs=pl.BlockSpec((MOE_TM, tn), lambda n, m, te, nu: (m, n)),
            scratch_shapes=[pltpu.VMEM((D_MODEL, tn), BF16), pltpu.VMEM((D_MODEL, tn), BF16)]),
        compiler_params=_cparams(("arbitrary", "arbitrary")),
        name="moe_gate_up",
    )(tile_e, n_used, xs, w_gu, w_gu, b_gu.reshape(N_EXPERTS, 1, -1), b_gu.reshape(N_EXPERTS, 1, -1))


def _down_body(te_ref, nu_ref, x_ref, w_ref, b_ref, o_ref, wb_ref):
    m = pl.program_id(1)

    @pl.when(m < nu_ref[0])
    def _():
        @pl.when(_first_of_expert(m, te_ref))
        def _():
            wb_ref[...] = w_ref[...].astype(BF16)

        o_ref[...] = jnp.dot(x_ref[...], wb_ref[...], preferred_element_type=F32) + b_ref[...]

    @pl.when(m >= nu_ref[0])
    def _():
        o_ref[...] = jnp.zeros_like(o_ref)


def _moe_down(tile_e, n_used, hact, w_down, b_down):
    tn = 1024
    nt = D_MODEL // tn

    def mm(m, nu):
        return jnp.minimum(m, nu[0] - 1)

    return pl.pallas_call(
        _down_body,
        out_shape=SDS((MOE_ROWS_PAD, D_MODEL), F32),
        grid_spec=pltpu.PrefetchScalarGridSpec(
            num_scalar_prefetch=2,
            grid=(nt, MOE_TILES),
            in_specs=[pl.BlockSpec((MOE_TM, D_FF), lambda n, m, te, nu: (mm(m, nu), 0)),
                      pl.BlockSpec((None, D_FF, tn), lambda n, m, te, nu: (te[mm(m, nu)], 0, n)),
                      pl.BlockSpec((None, 1, tn), lambda n, m, te, nu: (te[mm(m, nu)], 0, n))],
            out_specs=pl.BlockSpec((MOE_TM, tn), lambda n, m, te, nu: (m, n)),
            scratch_shapes=[pltpu.VMEM((D_FF, tn), BF16)]),
        compiler_params=_cparams(("arbitrary", "arbitrary")),
        name="moe_down",
    )(tile_e, n_used, hact, w_down, b_down.reshape(N_EXPERTS, 1, -1))


COMB_T = 128


def _combine_body(pos_ref, yb_ref, y1_ref, gates_ref, gt_ref, g_ref, b_ref, o_ref, buf_ref, sem):
    i = pl.program_id(0)
    base = i * COMB_T

    def row_copy(k, r, src):
        return pltpu.make_async_copy(yb_ref.at[pl.ds(src, 1), :], buf_ref.at[k, pl.ds(r, 1), :], sem)

    def issue(r, c):
        for k in range(TOP_K):
            row_copy(k, r, pos_ref[(base + r) * TOP_K + k]).start()
        return c

    lax.fori_loop(0, COMB_T, issue, 0)

    def drain(r, c):
        for k in range(TOP_K):
            row_copy(k, r, 0).wait()
        return c

    lax.fori_loop(0, COMB_T, drain, 0)
    gates = gates_ref[...]
    f = buf_ref[0] * gates[:, 0:1]
    for k in range(1, TOP_K):
        f = f + buf_ref[k] * gates[:, k:k + 1]
    z = DN_ALPHA * y1_ref[...] + gt_ref[0] * f
    o_ref[...] = _normalize(z) * g_ref[...] + b_ref[...]


def _moe_combine(pos, yb, y1, gates, gt, ln_g, ln_b, rows_per_group, tok_offset):
    n = y1.shape[0]
    row = pl.BlockSpec((COMB_T, D_MODEL), lambda i, pos: (i, 0))
    vec = pl.BlockSpec((1, D_MODEL), lambda i, pos: (0, 0))
    return pl.pallas_call(
        _combine_body,
        out_shape=SDS((n, D_MODEL), F32),
        grid_spec=pltpu.PrefetchScalarGridSpec(
            num_scalar_prefetch=1,
            grid=(n // COMB_T,),
            in_specs=[pl.BlockSpec(memory_space=pl.ANY), row,
                      pl.BlockSpec((COMB_T, TOP_K), lambda i, pos: (i, 0)),
                      _mod_spec(COMB_T, rows_per_group), vec, vec],
            out_specs=row,
            scratch_shapes=[pltpu.VMEM((TOP_K, COMB_T, D_MODEL), F32), pltpu.SemaphoreType.DMA(())]),
        compiler_params=_cparams(("arbitrary",)),
        name="moe_combine",
    )(pos, yb, y1, gates, gt, ln_g.reshape(1, -1), ln_b.reshape(1, -1))


def _route(logits):
    top_val, top_idx = lax.top_k(logits, TOP_K)
    gates = jax.nn.softmax(top_val, axis=-1)
    n_assign = N_TOK * TOP_K
    flat_e = top_idx.reshape(-1)
    order = jnp.argsort(flat_e)
    e_sorted = flat_e[order]
    counts = jnp.bincount(flat_e, length=N_EXPERTS)
    padded = (counts + MOE_TM - 1) // MOE_TM * MOE_TM
    pad_end = jnp.cumsum(padded)
    pad_start = pad_end - padded
    start = jnp.cumsum(counts) - counts
    dest = (pad_start[e_sorted] + jnp.arange(n_assign) - start[e_sorted]).astype(jnp.int32)
    row_tok = jnp.zeros((MOE_ROWS_PAD,), jnp.int32).at[dest].set((order // TOP_K).astype(jnp.int32))
    pos = jnp.zeros((n_assign,), jnp.int32).at[order].set(dest)
    tile_e = jnp.minimum(jnp.searchsorted(pad_end, jnp.arange(MOE_TILES) * MOE_TM, side='right'),
                         N_EXPERTS - 1).astype(jnp.int32)
    n_used = (pad_end[-1] // MOE_TM).astype(jnp.int32).reshape(1)
    return gates, row_tok, pos, tile_e, n_used


def _bucket_lut():
    n = np.arange(0, MAX_DISTANCE + 1)
    nf = np.maximum(n, 1).astype(np.float32)
    large = MAX_EXACT + (np.log(nf / MAX_EXACT) / math.log(MAX_DISTANCE / MAX_EXACT)
                         * (NUM_BUCKETS - MAX_EXACT)).astype(np.int32)
    return np.where(n < MAX_EXACT, n, np.minimum(large, NUM_BUCKETS - 1)).astype(np.int32)


_BUCKET_LUT = _bucket_lut()


def _bias_of(tbl, dist):
    b = tbl[:, jnp.clip(dist, 0, MAX_DISTANCE)]
    return b.reshape(NSA_KV_HEADS, NSA_GROUP, *dist.shape)


def _compress(rows, pe, w1, b1, w2, b2):
    bsz, length = rows.shape[:2]
    nch = length // CMP_STRIDE
    ncmp = nch - 1
    ch = rows[:, :nch * CMP_STRIDE].reshape(bsz, nch, CMP_STRIDE, 2, NSA_KV_HEADS, HEAD_DIM)
    pe_r = pe.reshape(2, 2, CMP_STRIDE, HEAD_DIM)
    w1_r = w1.reshape(2, 2, CMP_STRIDE, HEAD_DIM, HEAD_DIM)
    h = b1[None, None, :, None, :]
    for r in range(2):
        seg = ch[:, r:r + ncmp] + jnp.transpose(pe_r[:, r], (1, 0, 2))[:, :, None, :]
        h = h + jnp.einsum('bcjsgd,sjde->bcsge', seg, w1_r[:, r])
    return jnp.einsum('bcsgd,sde->bcsge', jax.nn.gelu(h), w2) + b2[None, None, :, None, :]


def _attend(q, k, v, bias, valid):
    s = jnp.einsum('btgjd,bsgd->bgjts', q, k).astype(F32) * ATTN_SCALE + bias
    s = jnp.where(valid, s, NEG_INF)
    p = jax.nn.softmax(s, axis=-1) * valid
    return jnp.einsum('bgjts,bsgd->btgjd', p, v), p


def _select(p_cmp, qpos, n_slc):
    nc = p_cmp.shape[-1]
    c0 = jnp.arange(nc) * CMP_STRIDE
    s0 = jnp.arange(n_slc) * SLC_BLOCK
    cover = ((c0[:, None] < s0[None, :] + SLC_BLOCK) & (c0[:, None] + CMP_BLOCK > s0[None, :])).astype(F32)
    imp = jnp.einsum('bgjtc,cs->bgts', p_cmp, cover)
    cur = (qpos // SLC_BLOCK)[:, None]
    blk = jnp.arange(n_slc)[None, :]
    forced = (blk == 0) | ((cur - blk >= 0) & (cur - blk < SLC_LOCAL))
    score = jnp.where(blk <= cur, imp + forced * FORCE_SCORE, -1.0)
    _, idx = lax.top_k(score, min(SLC_TOPN, n_slc))
    return (idx[..., None] == jnp.arange(n_slc)).any(-2)


def _nsa(q, qpos, cmp_rows, slc_k, slc_v, win_k, win_v, win_kpos, gate, lp, tbl):
    ckv = _compress(cmp_rows, lp['cmp_pe'], lp['cmp_w1'], lp['cmp_b1'], lp['cmp_w2'], lp['cmp_b2'])
    n_cmp = ckv.shape[1]
    d_cmp = qpos[:, None] - (jnp.arange(n_cmp) * CMP_STRIDE + CMP_BLOCK - 1)[None, :]
    o_cmp, p_cmp = _attend(q, ckv[:, :, 0], ckv[:, :, 1], _bias_of(tbl, d_cmp), d_cmp >= 0)
    length = slc_k.shape[1]
    n_slc = -(-length // SLC_BLOCK)
    sel = _select(p_cmp, qpos, n_slc)
    d_slc = qpos[:, None] - jnp.arange(length)[None, :]
    sel_keys = jnp.repeat(sel, SLC_BLOCK, axis=-1)[..., :length]
    o_slc, _ = _attend(q, slc_k, slc_v, _bias_of(tbl, d_slc), sel_keys[:, :, None] & (d_slc >= 0))
    d_win = qpos[:, None] - win_kpos[None, :]
    o_win, _ = _attend(q, win_k, win_v, _bias_of(tbl, d_win), (d_win >= 0) & (d_win < WINDOW))
    o = gate[:, :, 0] * o_cmp + gate[:, :, 1] * o_slc + gate[:, :, 2] * o_win
    return o.reshape(o.shape[0], o.shape[1], NSA_WIDTH)


def _gmlp(uv, lp):
    bsz, t = uv.shape[:2]
    uv = jax.nn.gelu(uv)
    u = uv[..., :GM_WIDTH].reshape(bsz, t, GM_HEADS, GM_HEAD_DIM)
    v = uv[..., GM_WIDTH:].reshape(bsz, t, GM_HEADS, GM_HEAD_DIM)
    v = _normalize(v) * lp['gm_ln_g'] + lp['gm_ln_b']
    cl = min(t, GM_CHUNK)
    w_s = jnp.where(jnp.tril(jnp.ones((cl, cl), bool)), lp['gm_ws'][:, :cl, :cl], 0)
    s = jnp.einsum('hij,bcjhd->bcihd', w_s, v.reshape(bsz, t // cl, cl, GM_HEADS, GM_HEAD_DIM))
    s = s + jnp.transpose(lp['gm_bs'][:, :cl])[None, None, :, :, None]
    return (u * s.reshape(bsz, t, GM_HEADS, GM_HEAD_DIM)).reshape(bsz, t, GM_WIDTH), v


def _split_proj(p, bsz, t):
    q = p[:, COL_Q:COL_KV].reshape(bsz, t, NSA_KV_HEADS, NSA_GROUP, HEAD_DIM)
    kv = p[:, COL_KV:COL_UV].reshape(bsz, t, 2 * N_BRANCH, NSA_KV_HEADS, HEAD_DIM)
    uv = p[:, COL_UV:COL_G].reshape(bsz, t, 2 * GM_WIDTH)
    g = jax.nn.sigmoid(p[:, COL_G:COL_G + GATE_COLS]).reshape(bsz, t, N_BRANCH, NSA_KV_HEADS, NSA_GROUP, 1)
    return q, kv, uv, g


def kernel(x_prompt, x_sample, cache_kv_pages, state_kv_window, page_table, c_prompt, c_sample, rel_bias_table,
           w_ada, b_ada, w_in, cmp_pe, cmp_w1, cmp_b1, cmp_w2, cmp_b2, gm_ln_g, gm_ln_b, gm_ws, gm_bs,
           g_norm_nsa, g_norm_gm, w_out, ln1_g, ln1_b, w_router, b_router, w_gu, b_gu, w_down, b_down,
           ln2_g, ln2_b):
    lp = {'cmp_pe': cmp_pe[0], 'cmp_w1': cmp_w1[0], 'cmp_b1': cmp_b1[0], 'cmp_w2': cmp_w2[0], 'cmp_b2': cmp_b2[0],
          'gm_ln_g': gm_ln_g[0], 'gm_ln_b': gm_ln_b[0], 'gm_ws': gm_ws[0], 'gm_bs': gm_bs[0]}
    xp = x_prompt.reshape(N_PROMPT, D_MODEL)
    xs = x_sample.reshape(N_SAMPLE, D_MODEL)

    c_all = jnp.concatenate([c_prompt, c_sample, jnp.zeros((4, D_MODEL), F32)], axis=0)
    mod = _ada(c_all, w_ada[0], b_ada[0])
    mod_p = [mod[:BATCH, i * D_MODEL:(i + 1) * D_MODEL].reshape(BATCH, 1, D_MODEL) for i in range(6)]

    def per_token(v, tile):
        return jnp.repeat(v, DEC_SEQ, axis=0).reshape(N_SAMPLE // tile, tile, D_MODEL)

    mod_s = [mod[BATCH:BATCH + DEC_BATCH, i * D_MODEL:(i + 1) * D_MODEL] for i in range(6)]

    w = w_in[0]
    w_r = jnp.concatenate([w[:, :COL_UV], w[:, COL_UV + GATE_COLS:], w[:, COL_UV:COL_UV + GATE_COLS],
                           jnp.zeros((D_MODEL, IN_COLS_PAD - COL_G - GATE_COLS), F32)], axis=1).astype(BF16)
    p_p = _inproj(xp, mod_p[1], mod_p[0], w_r, SEQ, 512)
    p_s = _inproj(xs, per_token(mod_s[1], 256), per_token(mod_s[0], 256), w_r, None, 256)

    tbl = rel_bias_table[jnp.asarray(_BUCKET_LUT)].T

    q, kv, uv, g = _split_proj(p_p, BATCH, SEQ)
    qpos = jnp.arange(SEQ)
    o_nsa_p = _nsa(q, qpos, kv[:, :, 0:2], kv[:, :, 2], kv[:, :, 3], kv[:, :, 4], kv[:, :, 5], qpos, g, lp, tbl)
    gm_p, _ = _gmlp(uv, lp)
    prompt_kv_rows = kv[:, :, 0:4][None]
    prompt_kv_window = kv[:, SEQ - WINDOW:, 4:6][None]

    q, kv, uv, g = _split_proj(p_s, DEC_BATCH, DEC_SEQ)
    past = cache_kv_pages[0][page_table].reshape(DEC_BATCH, PAST_LEN, 4, NSA_KV_HEADS, HEAD_DIM)
    full = jnp.concatenate([past, kv[:, :, 0:4]], axis=1)
    qpos = PAST_LEN + jnp.arange(DEC_SEQ)
    win_all = jnp.concatenate([state_kv_window[0], kv[:, :, 4:6]], axis=1)
    wbuf = state_kv_window.shape[2]
    kpos = PAST_LEN - wbuf + jnp.arange(wbuf + DEC_SEQ)
    o_nsa_s = _nsa(q, qpos, full[:, :, 0:2], full[:, :, 2], full[:, :, 3], win_all[:, :, 0], win_all[:, :, 1],
                   kpos, g, lp, tbl)
    gm_s, gmv_s = _gmlp(uv, lp)
    sample_kv_rows = kv[:, :, 0:4][None]
    sample_kv_window = win_all[:, DEC_SEQ:][None]
    sample_gmlp_v = gmv_s[None]

    o_nsa = jnp.concatenate([o_nsa_p.reshape(N_PROMPT, NSA_WIDTH), o_nsa_s.reshape(N_SAMPLE, NSA_WIDTH)], axis=0)
    gm = jnp.concatenate([gm_p.reshape(N_PROMPT, GM_WIDTH), gm_s.reshape(N_SAMPLE, GM_WIDTH)], axis=0)
    cat = _merge(o_nsa, gm, g_norm_nsa[0], g_norm_gm[0])
    mix = _matmul(cat, w_out[0].astype(BF16))
    wr = jnp.pad(w_router[0], ((0, 0), (0, 128 - N_EXPERTS))).astype(BF16)
    br = jnp.pad(b_router[0], (0, 128 - N_EXPERTS)).reshape(1, 128)
    y1_p, h_p, lg_p = _post1(xp, mix[:N_PROMPT], mod_p[2], mod_p[4], mod_p[3], ln1_g[0], ln1_b[0], wr, br, SEQ)
    y1_s, h_s, lg_s = _post1(xs, mix[N_PROMPT:], per_token(mod_s[2], POST_T), per_token(mod_s[4], POST_T),
                             per_token(mod_s[3], POST_T), ln1_g[0], ln1_b[0], wr, br, None)

    h = jnp.concatenate([h_p, h_s], axis=0)
    logits = jnp.concatenate([lg_p, lg_s], axis=0)[:, :N_EXPERTS]
    gates, row_tok, pos, tile_e, n_used = _route(logits)
    xg = _moe_gather(row_tok, n_used, h)
    hact = _moe_gate_up(tile_e, n_used, xg, w_gu[0], b_gu[0])
    yb = _moe_down(tile_e, n_used, hact, w_down[0], b_down[0])
    y_p = _moe_combine(pos[:N_PROMPT * TOP_K], yb, y1_p, gates[:N_PROMPT], mod_p[5], ln2_g[0], ln2_b[0], SEQ, 0)
    y_s = _moe_combine(pos[N_PROMPT * TOP_K:], yb, y1_s, gates[N_PROMPT:], per_token(mod_s[5], COMB_T),
                       ln2_g[0], ln2_b[0], None, 0)
    return (y_p.reshape(BATCH, SEQ, D_MODEL), y_s.reshape(DEC_BATCH, DEC_SEQ, D_MODEL),
            prompt_kv_rows, prompt_kv_window, sample_kv_rows, sample_kv_window, sample_gmlp_v)
```

```python
import functools
import math

import numpy as np
import jax
import jax.numpy as jnp
from jax import lax
from jax.experimental import pallas as pl
from jax.experimental.pallas import tpu as pltpu

F32 = jnp.float32
BF16 = jnp.bfloat16
SDS = jax.ShapeDtypeStruct

D_MODEL = 4096
BATCH = 4
SEQ = 2048
DEC_BATCH = 128
DEC_SEQ = 8
PAST_LEN = 2048
PAGE_SIZE = 128
N_PAGES = PAST_LEN // PAGE_SIZE
HEAD_DIM = 128
NSA_WIDTH = D_MODEL // 2
NSA_HEADS = NSA_WIDTH // HEAD_DIM
NSA_KV_HEADS = 4
NSA_GROUP = NSA_HEADS // NSA_KV_HEADS
N_BRANCH = 3
CMP_BLOCK = 32
CMP_STRIDE = 16
SLC_BLOCK = 64
SLC_TOPN = 16
SLC_LOCAL = 2
WINDOW = 512
FORCE_SCORE = 1e4
ATTN_SCALE = HEAD_DIM ** -0.5
NEG_INF = -1e30
GM_WIDTH = D_MODEL - NSA_WIDTH
GM_HEAD_DIM = 128
GM_HEADS = GM_WIDTH // GM_HEAD_DIM
GM_CHUNK = 128
KV_COLS = N_BRANCH * 2 * NSA_KV_HEADS * HEAD_DIM
GATE_COLS = N_BRANCH * NSA_HEADS
NUM_BUCKETS = 32
MAX_EXACT = NUM_BUCKETS // 2
MAX_DISTANCE = 128
N_EXPERTS = 32
TOP_K = 4
D_FF = D_MODEL
SWIGLU_LIMIT = 7.0
SWIGLU_ALPHA = 1.702
DN_ALPHA = 2.0 ** 0.25
LN_EPS = 1e-5

N_PROMPT = BATCH * SEQ
N_SAMPLE = DEC_BATCH * DEC_SEQ
N_TOK = N_PROMPT + N_SAMPLE

COL_Q = 0
COL_U = NSA_WIDTH
COL_V = COL_U + GM_WIDTH
COL_KV = COL_V + GM_WIDTH
COL_G = COL_KV + KV_COLS
LANE = 128
IN_COLS_PAD = COL_G + NSA_KV_HEADS * LANE
QBLK = 128
N_QBLK = SEQ // QBLK
N_CMP_PAD = 128
SLC_PAD = 40
S_KEYS = PAST_LEN + LANE
W_KEYS = WINDOW + LANE
PAGE_SLOTS = 4 * NSA_KV_HEADS
WIN_SLOTS = 2 * NSA_KV_HEADS

VMEM_LIMIT = 56 * 1024 * 1024


def _cparams(sem):
    return pltpu.CompilerParams(dimension_semantics=sem, vmem_limit_bytes=VMEM_LIMIT)


def _normalize(x):
    xc = x - jnp.mean(x, axis=-1, keepdims=True)
    return xc * lax.rsqrt(jnp.mean(xc * xc, axis=-1, keepdims=True) + LN_EPS)


def _ada_body(c_ref, w_ref, b_ref, o_ref):
    c = c_ref[...]
    a = (c * jax.nn.sigmoid(c)).astype(BF16)
    o_ref[...] = jnp.dot(a, w_ref[...].astype(BF16), preferred_element_type=F32) + b_ref[...]


def _ada(c_all, w_ada, b_ada):
    rows = c_all.shape[0]
    tn = 512
    return pl.pallas_call(
        _ada_body,
        out_shape=SDS((rows, 6 * D_MODEL), F32),
        grid=(6 * D_MODEL // tn,),
        in_specs=[pl.BlockSpec((rows, D_MODEL), lambda j: (0, 0)),
                  pl.BlockSpec((D_MODEL, tn), lambda j: (0, j)),
                  pl.BlockSpec((1, tn), lambda j: (0, j))],
        out_specs=pl.BlockSpec((rows, tn), lambda j: (0, j)),
        compiler_params=_cparams(("arbitrary",)),
        name="ada",
    )(c_all, w_ada, b_ada.reshape(1, -1))


def _inproj_body(x_ref, sc_ref, sh_ref, w_ref, o_ref, m_ref):
    @pl.when(pl.program_id(1) == 0)
    def _():
        rows = x_ref.shape[0]
        per_token = sc_ref.shape[1] != 1
        for r0 in range(0, rows, 128):
            rs = slice(r0, r0 + 128)
            sc = sc_ref[0, rs] if per_token else sc_ref[0]
            sh = sh_ref[0, rs] if per_token else sh_ref[0]
            m_ref[rs] = (_normalize(x_ref[rs]) * (1.0 + sc) + sh).astype(BF16)

    o_ref[...] = jnp.dot(m_ref[...], w_ref[...], preferred_element_type=F32)


def _mod_spec(tile_rows, rows_per_group):
    if rows_per_group is None:
        return pl.BlockSpec((1, tile_rows, D_MODEL), lambda i, *_: (i, 0, 0))
    per = rows_per_group // tile_rows
    return pl.BlockSpec((1, 1, D_MODEL), lambda i, *_: (i // per, 0, 0))


def _inproj(x, sc, sh, w_bf16, rows_per_group, tm):
    n = x.shape[0]
    tn = 512
    ncols = w_bf16.shape[1]
    return pl.pallas_call(
        _inproj_body,
        out_shape=SDS((n, ncols), F32),
        grid=(n // tm, ncols // tn),
        in_specs=[pl.BlockSpec((tm, D_MODEL), lambda i, j: (i, 0)),
                  _mod_spec(tm, rows_per_group), _mod_spec(tm, rows_per_group),
                  pl.BlockSpec((D_MODEL, tn), lambda i, j: (0, j))],
        out_specs=pl.BlockSpec((tm, tn), lambda i, j: (i, j)),
        scratch_shapes=[pltpu.VMEM((tm, D_MODEL), BF16)],
        compiler_params=_cparams(("arbitrary", "arbitrary")),
        name="inproj",
    )(x, sc, sh, w_bf16)


def _merge_body(o_ref, gm_ref, gn_ref, gg_ref, cat_ref):
    o = o_ref[...]
    gm = gm_ref[...]
    on = o * lax.rsqrt(jnp.mean(o * o, axis=-1, keepdims=True) + LN_EPS) * gn_ref[...]
    gn = gm * lax.rsqrt(jnp.mean(gm * gm, axis=-1, keepdims=True) + LN_EPS) * gg_ref[...]
    cat_ref[:, :NSA_WIDTH] = on.astype(BF16)
    cat_ref[:, NSA_WIDTH:] = gn.astype(BF16)


def _merge(o_nsa, gm_out, g_nsa, g_gm):
    n = o_nsa.shape[0]
    tr = 512
    return pl.pallas_call(
        _merge_body,
        out_shape=SDS((n, D_MODEL), BF16),
        grid=(n // tr,),
        in_specs=[pl.BlockSpec((tr, NSA_WIDTH), lambda i: (i, 0)),
                  pl.BlockSpec((tr, GM_WIDTH), lambda i: (i, 0)),
                  pl.BlockSpec((1, NSA_WIDTH), lambda i: (0, 0)),
                  pl.BlockSpec((1, GM_WIDTH), lambda i: (0, 0))],
        out_specs=pl.BlockSpec((tr, D_MODEL), lambda i: (i, 0)),
        compiler_params=_cparams(("arbitrary",)),
        name="merge",
    )(o_nsa, gm_out, g_nsa.reshape(1, -1), g_gm.reshape(1, -1))


def _mm_body(x_ref, w_ref, o_ref):
    o_ref[...] = jnp.dot(x_ref[...], w_ref[...], preferred_element_type=F32)


def _matmul(x_bf16, w_bf16, tm=512, tn=1024):
    n, k = x_bf16.shape
    ncols = w_bf16.shape[1]
    return pl.pallas_call(
        _mm_body,
        out_shape=SDS((n, ncols), F32),
        grid=(n // tm, ncols // tn),
        in_specs=[pl.BlockSpec((tm, k), lambda i, j: (i, 0)),
                  pl.BlockSpec((k, tn), lambda i, j: (0, j))],
        out_specs=pl.BlockSpec((tm, tn), lambda i, j: (i, j)),
        compiler_params=_cparams(("arbitrary", "arbitrary")),
        name="outproj",
    )(x_bf16, w_bf16)


POST_T = 128


def _post1_body(x_ref, mix_ref, gt_ref, sc_ref, sh_ref, g_ref, b_ref, wr_ref, br_ref,
                y_ref, h_ref, lg_ref):
    z = DN_ALPHA * x_ref[...] + gt_ref[0] * mix_ref[...]
    y = _normalize(z) * g_ref[...] + b_ref[...]
    y_ref[...] = y
    h = _normalize(y) * (1.0 + sc_ref[0]) + sh_ref[0]
    h_ref[...] = h
    lg_ref[...] = jnp.dot(h.astype(BF16), wr_ref[...], preferred_element_type=F32) + br_ref[...]


def _post1(x, mix, gt, sc, sh, ln_g, ln_b, wr_bf16, br, rows_per_group):
    n = x.shape[0]
    tr = POST_T
    row = pl.BlockSpec((tr, D_MODEL), lambda i: (i, 0))
    vec = pl.BlockSpec((1, D_MODEL), lambda i: (0, 0))
    ms = _mod_spec(tr, rows_per_group)
    return pl.pallas_call(
        _post1_body,
        out_shape=(SDS((n, D_MODEL), F32), SDS((n, D_MODEL), F32), SDS((n, 128), F32)),
        grid=(n // tr,),
        in_specs=[row, row, ms, ms, ms, vec, vec,
                  pl.BlockSpec((D_MODEL, 128), lambda i: (0, 0)),
                  pl.BlockSpec((1, 128), lambda i: (0, 0))],
        out_specs=(row, row, pl.BlockSpec((tr, 128), lambda i: (i, 0))),
        compiler_params=_cparams(("arbitrary",)),
        name="post1",
    )(x, mix, gt, sc, sh, ln_g.reshape(1, -1), ln_b.reshape(1, -1), wr_bf16, br)


MOE_TM = 512
MOE_TILES = -(-(N_TOK * TOP_K + N_EXPERTS * (MOE_TM - 1)) // MOE_TM)
MOE_ROWS_PAD = MOE_TILES * MOE_TM


def _gather_body(tok_ref, nu_ref, h_ref, o_ref, buf_ref, sem):
    i = pl.program_id(0)

    @pl.when(i < nu_ref[0])
    def _():
        base = i * MOE_TM

        def row_copy(r, tok):
            return pltpu.make_async_copy(h_ref.at[pl.ds(tok, 1), :], buf_ref.at[pl.ds(r, 1), :], sem)

        def issue(r, c):
            row_copy(r, tok_ref[base + r]).start()
            return c

        lax.fori_loop(0, MOE_TM, issue, 0)

        def drain(r, c):
            row_copy(r, 0).wait()
            return c

        lax.fori_loop(0, MOE_TM, drain, 0)
        o_ref[...] = buf_ref[...].astype(BF16)

    @pl.when(i >= nu_ref[0])
    def _():
        o_ref[...] = jnp.zeros_like(o_ref)


def _moe_gather(row_tok, n_used, h):
    return pl.pallas_call(
        _gather_body,
        out_shape=SDS((MOE_ROWS_PAD, D_MODEL), BF16),
        grid_spec=pltpu.PrefetchScalarGridSpec(
            num_scalar_prefetch=2,
            grid=(MOE_TILES,),
            in_specs=[pl.BlockSpec(memory_space=pl.ANY)],
            out_specs=pl.BlockSpec((MOE_TM, D_MODEL), lambda i, tok, nu: (i, 0)),
            scratch_shapes=[pltpu.VMEM((MOE_TM, D_MODEL), F32), pltpu.SemaphoreType.DMA(())]),
        compiler_params=_cparams(("arbitrary",)),
        name="moe_gather",
    )(row_tok, n_used, h)


def _first_of_expert(m, te_ref):
    return (m == 0) | (te_ref[m] != te_ref[jnp.maximum(m - 1, 0)])


def _expert_weights(te_ref, rid_ref, rexp_ref, nr_ref, w_hbm, wbuf, sem, col_offsets, tn, casted):
    n = pl.program_id(0)
    m = pl.program_id(1)

    def copies(e, nn, slot):
        return [pltpu.make_async_copy(w_hbm.at[e, :, pl.ds(pl.multiple_of(off + nn * tn, tn), tn)],
                                      wbuf.at[slot, i], sem.at[slot]) for i, off in enumerate(col_offsets)]

    @pl.when(_first_of_expert(m, te_ref))
    def _():
        n_runs = nr_ref[0]
        k = rid_ref[m]
        c = n * n_runs + k
        slot = c % 2

        @pl.when(c == 0)
        def _():
            for cp in copies(te_ref[0], 0, 0):
                cp.start()

        wrap = k + 1 >= n_runs
        k2 = jnp.where(wrap, 0, k + 1)
        n2 = jnp.where(wrap, n + 1, n)

        @pl.when(n2 < pl.num_programs(0))
        def _():
            for cp in copies(rexp_ref[k2], n2, 1 - slot):
                cp.start()

        for cp in copies(0, 0, slot):
            cp.wait()
        for i, dst in enumerate(casted):
            dst[...] = wbuf[slot, i].astype(BF16)


def _gu_body(te_ref, nu_ref, rid_ref, rexp_ref, nr_ref, x_ref, w_hbm, bg_ref, bl_ref, o_ref,
             wbuf, wgb_ref, wlb_ref, sem):
    m = pl.program_id(1)

    @pl.when(m < nu_ref[0])
    def _():
        _expert_weights(te_ref, rid_ref, rexp_ref, nr_ref, w_hbm, wbuf, sem, (0, D_FF), GU_TN,
                        (wgb_ref, wlb_ref))
        x = x_ref[...]
        gl = jnp.dot(x, wgb_ref[...], preferred_element_type=F32) + bg_ref[...]
        lin = jnp.dot(x, wlb_ref[...], preferred_element_type=F32) + bl_ref[...]
        gl = jnp.minimum(gl, SWIGLU_LIMIT)
        lin = jnp.clip(lin, -SWIGLU_LIMIT, SWIGLU_LIMIT)
        o_ref[...] = ((lin + 1.0) * (gl * jax.nn.sigmoid(SWIGLU_ALPHA * gl))).astype(BF16)

    @pl.when(m >= nu_ref[0])
    def _():
        o_ref[...] = jnp.zeros_like(o_ref)


GU_TN = 512
DOWN_TN = 1024


def _tile_idx(m, nu):
    return jnp.minimum(m, nu[0] - 1)


def _moe_gate_up(route, xs, w_gu, b_gu):
    tn = GU_TN
    nt = D_FF // tn
    bias = b_gu.reshape(N_EXPERTS, 1, -1)
    return pl.pallas_call(
        _gu_body,
        out_shape=SDS((MOE_ROWS_PAD, D_FF), BF16),
        grid_spec=pltpu.PrefetchScalarGridSpec(
            num_scalar_prefetch=5,
            grid=(nt, MOE_TILES),
            in_specs=[pl.BlockSpec((MOE_TM, D_MODEL), lambda n, m, te, nu, *_: (_tile_idx(m, nu), 0)),
                      pl.BlockSpec(memory_space=pl.ANY),
                      pl.BlockSpec((None, 1, tn), lambda n, m, te, nu, *_: (te[_tile_idx(m, nu)], 0, n)),
                      pl.BlockSpec((None, 1, tn), lambda n, m, te, nu, *_: (te[_tile_idx(m, nu)], 0, n + nt))],
            out_specs=pl.BlockSpec((MOE_TM, tn), lambda n, m, *_: (m, n)),
            scratch_shapes=[pltpu.VMEM((2, 2, D_MODEL, tn), F32), pltpu.VMEM((D_MODEL, tn), BF16),
                            pltpu.VMEM((D_MODEL, tn), BF16), pltpu.SemaphoreType.DMA((2,))]),
        compiler_params=_cparams(("arbitrary", "arbitrary")),
        name="moe_gate_up",
    )(*route, xs, w_gu, bias, bias)


def _down_body(te_ref, nu_ref, rid_ref, rexp_ref, nr_ref, x_ref, w_hbm, b_ref, o_ref, wbuf, wb_ref, sem):
    m = pl.program_id(1)

    @pl.when(m < nu_ref[0])
    def _():
        _expert_weights(te_ref, rid_ref, rexp_ref, nr_ref, w_hbm, wbuf, sem, (0,), DOWN_TN, (wb_ref,))
        o_ref[...] = jnp.dot(x_ref[...], wb_ref[...], preferred_element_type=F32) + b_ref[...]

    @pl.when(m >= nu_ref[0])
    def _():
        o_ref[...] = jnp.zeros_like(o_ref)


def _moe_down(route, hact, w_down, b_down):
    tn = DOWN_TN
    nt = D_MODEL // tn
    return pl.pallas_call(
        _down_body,
        out_shape=SDS((MOE_ROWS_PAD, D_MODEL), F32),
        grid_spec=pltpu.PrefetchScalarGridSpec(
            num_scalar_prefetch=5,
            grid=(nt, MOE_TILES),
            in_specs=[pl.BlockSpec((MOE_TM, D_FF), lambda n, m, te, nu, *_: (_tile_idx(m, nu), 0)),
                      pl.BlockSpec(memory_space=pl.ANY),
                      pl.BlockSpec((None, 1, tn), lambda n, m, te, nu, *_: (te[_tile_idx(m, nu)], 0, n))],
            out_specs=pl.BlockSpec((MOE_TM, tn), lambda n, m, *_: (m, n)),
            scratch_shapes=[pltpu.VMEM((2, 1, D_FF, tn), F32), pltpu.VMEM((D_FF, tn), BF16),
                            pltpu.SemaphoreType.DMA((2,))]),
        compiler_params=_cparams(("arbitrary", "arbitrary")),
        name="moe_down",
    )(*route, hact, w_down, b_down.reshape(N_EXPERTS, 1, -1))


COMB_T = 128


def _combine_body(pos_ref, yb_ref, y1_ref, gates_ref, gt_ref, g_ref, b_ref, o_ref, buf_ref, sem):
    i = pl.program_id(0)
    base = i * COMB_T

    def row_copy(k, r, src):
        return pltpu.make_async_copy(yb_ref.at[pl.ds(src, 1), :], buf_ref.at[k, pl.ds(r, 1), :], sem)

    def issue(r, c):
        for k in range(TOP_K):
            row_copy(k, r, pos_ref[(base + r) * TOP_K + k]).start()
        return c

    lax.fori_loop(0, COMB_T, issue, 0)

    def drain(r, c):
        for k in range(TOP_K):
            row_copy(k, r, 0).wait()
        return c

    lax.fori_loop(0, COMB_T, drain, 0)
    gates = gates_ref[...]
    f = buf_ref[0] * gates[:, 0:1]
    for k in range(1, TOP_K):
        f = f + buf_ref[k] * gates[:, k:k + 1]
    z = DN_ALPHA * y1_ref[...] + gt_ref[0] * f
    o_ref[...] = _normalize(z) * g_ref[...] + b_ref[...]


def _moe_combine(pos, yb, y1, gates, gt, ln_g, ln_b, rows_per_group, tok_offset):
    n = y1.shape[0]
    row = pl.BlockSpec((COMB_T, D_MODEL), lambda i, pos: (i, 0))
    vec = pl.BlockSpec((1, D_MODEL), lambda i, pos: (0, 0))
    return pl.pallas_call(
        _combine_body,
        out_shape=SDS((n, D_MODEL), F32),
        grid_spec=pltpu.PrefetchScalarGridSpec(
            num_scalar_prefetch=1,
            grid=(n // COMB_T,),
            in_specs=[pl.BlockSpec(memory_space=pl.ANY), row,
                      pl.BlockSpec((COMB_T, TOP_K), lambda i, pos: (i, 0)),
                      _mod_spec(COMB_T, rows_per_group), vec, vec],
            out_specs=row,
            scratch_shapes=[pltpu.VMEM((TOP_K, COMB_T, D_MODEL), F32), pltpu.SemaphoreType.DMA(())]),
        compiler_params=_cparams(("arbitrary",)),
        name="moe_combine",
    )(pos, yb, y1, gates, gt, ln_g.reshape(1, -1), ln_b.reshape(1, -1))


def _route(logits):
    top_val, top_idx = lax.top_k(logits, TOP_K)
    gates = jax.nn.softmax(top_val, axis=-1)
    n_assign = N_TOK * TOP_K
    flat_e = top_idx.reshape(-1)
    order = jnp.argsort(flat_e)
    e_sorted = flat_e[order]
    counts = jnp.bincount(flat_e, length=N_EXPERTS)
    padded = (counts + MOE_TM - 1) // MOE_TM * MOE_TM
    pad_end = jnp.cumsum(padded)
    pad_start = pad_end - padded
    start = jnp.cumsum(counts) - counts
    dest = (pad_start[e_sorted] + jnp.arange(n_assign) - start[e_sorted]).astype(jnp.int32)
    row_tok = jnp.zeros((MOE_ROWS_PAD,), jnp.int32).at[dest].set((order // TOP_K).astype(jnp.int32))
    pos = jnp.zeros((n_assign,), jnp.int32).at[order].set(dest)
    tile_e = jnp.minimum(jnp.searchsorted(pad_end, jnp.arange(MOE_TILES) * MOE_TM, side='right'),
                         N_EXPERTS - 1).astype(jnp.int32)
    n_used = (pad_end[-1] // MOE_TM).astype(jnp.int32).reshape(1)
    tiles = jnp.arange(MOE_TILES)
    first = ((tiles == 0) | (tile_e != jnp.roll(tile_e, 1))) & (tiles < n_used[0])
    run_id = (jnp.cumsum(first) - 1).astype(jnp.int32)
    run_expert = jnp.zeros((N_EXPERTS,), jnp.int32).at[jnp.where(first, run_id, N_EXPERTS)].set(tile_e, mode='drop')
    n_runs = jnp.sum(first).astype(jnp.int32).reshape(1)
    return gates, row_tok, pos, (tile_e, n_used, run_id, run_expert, n_runs)


def _bucket_lut():
    n = np.arange(0, MAX_DISTANCE + 1)
    nf = np.maximum(n, 1).astype(np.float32)
    large = MAX_EXACT + (np.log(nf / MAX_EXACT) / math.log(MAX_DISTANCE / MAX_EXACT)
                         * (NUM_BUCKETS - MAX_EXACT)).astype(np.int32)
    return np.where(n < MAX_EXACT, n, np.minimum(large, NUM_BUCKETS - 1)).astype(np.int32)


_BUCKET_LUT = _bucket_lut()


def _bias_at(tbl, dist):
    return tbl[:, jnp.clip(dist, 0, MAX_DISTANCE)]


def _cover_t(n_slc):
    s0 = np.arange(LANE)[:, None] * SLC_BLOCK
    c0 = np.arange(LANE)[None, :] * CMP_STRIDE
    cov = (c0 < s0 + SLC_BLOCK) & (c0 + CMP_BLOCK > s0) & (np.arange(LANE)[None, :] < N_CMP_PAD - 1)
    return jnp.asarray(cov & (np.arange(LANE)[:, None] < n_slc), BF16)


def _prompt_tables(tbl):
    i = np.arange(QBLK)[:, None]
    k = np.arange(QBLK)[None, :]
    near = jnp.stack([_bias_at(tbl, i - k), _bias_at(tbl, QBLK + i - k), _bias_at(tbl, 2 * QBLK + i - k)])
    t = np.arange(SEQ)[:, None]
    c = np.arange(N_CMP_PAD)[None, :]
    b_cmp = _bias_at(tbl, t - (c * CMP_STRIDE + CMP_BLOCK - 1))
    key = np.arange(SEQ).reshape(N_QBLK, 1, QBLK)
    expand = jnp.asarray(key // SLC_BLOCK == np.arange(LANE)[None, :, None], BF16)
    return near, b_cmp, expand


def _sample_tables(tbl):
    t = np.arange(DEC_SEQ)[:, None]
    qpos = PAST_LEN + t

    def rows(x):
        return x.reshape(NSA_KV_HEADS, NSA_GROUP * DEC_SEQ, x.shape[-1])

    def tile(m):
        return jnp.asarray(np.tile(m, (NSA_GROUP, 1)), F32)

    c = np.arange(N_CMP_PAD)[None, :]
    b_cmp = rows(_bias_at(tbl, qpos - (c * CMP_STRIDE + CMP_BLOCK - 1)))
    ks = np.arange(S_KEYS)[None, :]
    b_slc = rows(_bias_at(tbl, qpos - ks))
    ok_slc = tile((ks <= qpos) & (ks < PAST_LEN + DEC_SEQ))
    kw = np.arange(W_KEYS)[None, :]
    d_win = WINDOW + t - kw
    b_win = rows(_bias_at(tbl, d_win))
    ok_win = tile((d_win >= 0) & (d_win < WINDOW) & (kw < WINDOW + DEC_SEQ))
    expand = jnp.asarray(np.arange(S_KEYS)[None, :] // SLC_BLOCK == np.arange(LANE)[:, None], BF16)
    return b_cmp, b_slc, b_win, ok_slc, ok_win, expand


_NT = (((1,), (1,)), ((), ()))


def _compress_rows(load, s, w1_ref, pe_ref, b1_ref, w2_ref, b2_ref):
    acc = [None, None]
    for jp in range(CMP_STRIDE // 2):
        xa, xb = load(2 * jp), load(2 * jp + 1)
        for r in range(2):
            ia = r * CMP_STRIDE + 2 * jp
            lhs = jnp.concatenate([xa + pe_ref[s, pl.ds(ia, 1), :], xb + pe_ref[s, pl.ds(ia + 1, 1), :]],
                                  axis=1).astype(BF16)
            d = jnp.dot(lhs, w1_ref[s, r, jp], preferred_element_type=F32)
            acc[r] = d if acc[r] is None else acc[r] + d
    h = b1_ref[s] + acc[0] + pltpu.roll(acc[1], NSA_KV_HEADS * N_CMP_PAD - 1, 0)
    return jnp.dot(jax.nn.gelu(h).astype(BF16), w2_ref[s], preferred_element_type=F32) + b2_ref[s]


def _cmp_prompt_body(*refs):
    n_x = 2 * NSA_KV_HEADS
    x_refs = refs[:n_x]
    w1_ref, pe_ref, b1_ref, w2_ref, b2_ref, o_ref = refs[n_x:]
    for s in range(2):
        def load(j, s=s):
            return jnp.concatenate(
                [x_refs[s * NSA_KV_HEADS + g][pl.ds(j, N_CMP_PAD, stride=CMP_STRIDE), :]
                 for g in range(NSA_KV_HEADS)], axis=0)

        out = _compress_rows(load, s, w1_ref, pe_ref, b1_ref, w2_ref, b2_ref)
        for g in range(NSA_KV_HEADS):
            o_ref[0, s, g] = out[g * N_CMP_PAD:(g + 1) * N_CMP_PAD]


def _full_spec(shape):
    nd = len(shape)
    return pl.BlockSpec(shape, lambda *_: (0,) * nd)


def _compress_prompt(p_p, cw):
    return pl.pallas_call(
        _cmp_prompt_body,
        out_shape=SDS((BATCH, 2, NSA_KV_HEADS, N_CMP_PAD, LANE), F32),
        grid=(BATCH,),
        in_specs=[pl.BlockSpec((SEQ, LANE), lambda b, c=COL_KV // LANE + sg: (b, c))
                  for sg in range(2 * NSA_KV_HEADS)]
        + [_full_spec(a.shape) for a in cw],
        out_specs=pl.BlockSpec((1, 2, NSA_KV_HEADS, N_CMP_PAD, LANE), lambda b: (b, 0, 0, 0, 0)),
        compiler_params=_cparams(("arbitrary",)),
        name="compress_prompt",
    )(*([p_p] * (2 * NSA_KV_HEADS)), *cw)


def _select_blocks(psum, cov_t, t_abs, n_slc):
    hi = psum.astype(BF16)
    lo = (psum - hi.astype(F32)).astype(BF16)
    imp = (lax.dot_general(cov_t, hi, _NT, preferred_element_type=F32)
           + lax.dot_general(cov_t, lo, _NT, preferred_element_type=F32))[:SLC_PAD]
    blk = lax.broadcasted_iota(jnp.int32, (SLC_PAD, LANE), 0)
    cur = jnp.right_shift(t_abs, SLC_BLOCK.bit_length() - 1)
    forced = (blk == 0) | ((cur - blk >= 0) & (cur - blk < SLC_LOCAL))
    score = jnp.where(blk <= cur, imp + jnp.where(forced, FORCE_SCORE, 0.0), -1.0)
    score = jnp.where(blk < n_slc, score, -2.0)
    ahead = jnp.zeros((SLC_PAD, LANE), F32)
    for s2 in range(n_slc):
        row = score[s2:s2 + 1, :]
        ahead = ahead + jnp.where((row > score) | ((row == score) & (blk > s2)), 1.0, 0.0)
    sel_t = jnp.where(ahead < SLC_TOPN, 1.0, 0.0)
    sel_t = jnp.concatenate([sel_t, jnp.zeros((LANE - SLC_PAD, LANE), F32)], axis=0)
    return sel_t.T


def _nsa_prompt_body(q_ref, ckv_ref, ks_ref, vs_ref, kw_ref, vw_ref, gl_ref, bc_ref, b3_ref, ex_ref, cov_ref,
                     o_ref, mask_ref):
    qb = pl.program_id(2)
    t0 = qb * QBLK
    q = q_ref[...]
    qs = jnp.concatenate([q[:, j * LANE:(j + 1) * LANE] for j in range(NSA_GROUP)], axis=0).astype(BF16)
    row_t = t0 + lax.broadcasted_iota(jnp.int32, (QBLK, LANE), 0)
    lane_i = lax.broadcasted_iota(jnp.int32, (QBLK, LANE), 1)

    kc = ckv_ref[0, 0, 0].astype(BF16)
    vc = ckv_ref[0, 1, 0].astype(BF16)
    s = lax.dot_general(qs, kc, _NT, preferred_element_type=F32) * ATTN_SCALE
    s = s.reshape(NSA_GROUP, QBLK, LANE) + bc_ref[...]
    valid = (row_t >= lane_i * CMP_STRIDE + (CMP_BLOCK - 1))[None]
    s = jnp.where(valid, s, NEG_INF)
    e = jnp.where(valid, jnp.exp(s - jnp.max(s, axis=-1, keepdims=True)), 0.0)
    den = jnp.sum(e, axis=-1, keepdims=True)
    p = e / jnp.where(den > 0.0, den, 1.0)
    o_cmp = jnp.dot(p.reshape(NSA_GROUP * QBLK, LANE).astype(BF16), vc,
                    preferred_element_type=F32).reshape(NSA_GROUP, QBLK, LANE)

    t_abs = t0 + lax.broadcasted_iota(jnp.int32, (SLC_PAD, LANE), 1)
    sel = _select_blocks(p[0] + p[1] + p[2] + p[3], cov_ref[...], t_abs, SEQ // SLC_BLOCK).astype(BF16)
    for kt in range(N_QBLK):
        mask_ref[kt] = jnp.dot(sel, ex_ref[kt], preferred_element_type=F32)

    def attend(k_ref, v_ref, lo, use_sel):
        def step(kt, carry):
            m, l, acc = carry
            k0 = pl.multiple_of(kt * QBLK, QBLK)
            k = k_ref[pl.ds(k0, QBLK), :].astype(BF16)
            v = v_ref[pl.ds(k0, QBLK), :].astype(BF16)
            sc = lax.dot_general(qs, k, _NT, preferred_element_type=F32) * ATTN_SCALE
            sc = sc.reshape(NSA_GROUP, QBLK, LANE) + b3_ref[jnp.minimum(qb - kt, 2)]
            dist = row_t - (k0 + lane_i)
            ok = (dist >= 0) & ((mask_ref[kt] > 0.5) if use_sel else (dist < WINDOW))
            sc = jnp.where(ok[None], sc, NEG_INF)
            m_new = jnp.maximum(m, jnp.max(sc, axis=-1, keepdims=True))
            alpha = jnp.exp(m - m_new)
            pr = jnp.exp(sc - m_new)
            l = alpha * l + jnp.sum(pr, axis=-1, keepdims=True)
            pv = jnp.dot(pr.reshape(NSA_GROUP * QBLK, LANE).astype(BF16), v, preferred_element_type=F32)
            return m_new, l, alpha * acc + pv.reshape(NSA_GROUP, QBLK, LANE)

        init = (jnp.full((NSA_GROUP, QBLK, 1), NEG_INF, F32), jnp.zeros((NSA_GROUP, QBLK, 1), F32),
                jnp.zeros((NSA_GROUP, QBLK, LANE), F32))
        _, l, acc = lax.fori_loop(lo, qb + 1, step, init)
        return acc / l

    o_slc = attend(ks_ref, vs_ref, 0, True)
    o_win = attend(kw_ref, vw_ref, jnp.maximum(qb - WINDOW // QBLK, 0), False)

    gs = jax.nn.sigmoid(gl_ref[...])
    for j in range(NSA_GROUP):
        o_ref[:, j * LANE:(j + 1) * LANE] = (
            gs[:, j:j + 1] * o_cmp[j] + gs[:, NSA_GROUP + j:NSA_GROUP + j + 1] * o_slc[j]
            + gs[:, 2 * NSA_GROUP + j:2 * NSA_GROUP + j + 1] * o_win[j])


def _nsa_prompt(p_p, ckv, near, b_cmp, expand, cov_t):
    kv_blk = COL_KV // LANE

    def kv_spec(branch):
        return pl.BlockSpec((SEQ, LANE), lambda b, g, i: (b, kv_blk + branch * NSA_KV_HEADS + g))

    return pl.pallas_call(
        _nsa_prompt_body,
        out_shape=SDS((N_PROMPT, NSA_WIDTH), F32),
        grid=(BATCH, NSA_KV_HEADS, N_QBLK),
        in_specs=[pl.BlockSpec((QBLK, NSA_GROUP * LANE), lambda b, g, i: (b * N_QBLK + i, g)),
                  pl.BlockSpec((1, 2, 1, N_CMP_PAD, LANE), lambda b, g, i: (b, 0, g, 0, 0)),
                  kv_spec(2), kv_spec(3), kv_spec(4), kv_spec(5),
                  pl.BlockSpec((QBLK, LANE), lambda b, g, i: (b * N_QBLK + i, COL_G // LANE + g)),
                  pl.BlockSpec((NSA_GROUP, QBLK, LANE), lambda b, g, i: (g, i, 0)),
                  pl.BlockSpec((3, NSA_GROUP, QBLK, LANE), lambda b, g, i: (0, g, 0, 0)),
                  _full_spec(expand.shape), _full_spec(cov_t.shape)],
        out_specs=pl.BlockSpec((QBLK, NSA_GROUP * LANE), lambda b, g, i: (b * N_QBLK + i, g)),
        scratch_shapes=[pltpu.VMEM((N_QBLK, QBLK, LANE), F32)],
        compiler_params=_cparams(("arbitrary", "arbitrary", "arbitrary")),
        name="nsa_prompt",
    )(p_p, ckv, p_p, p_p, p_p, p_p, p_p, b_cmp, near, expand, cov_t)


def _softmax_rows(s):
    e = jnp.exp(s - jnp.max(s, axis=-1, keepdims=True))
    return e / jnp.sum(e, axis=-1, keepdims=True)


def _nsa_sample_body(pt_ref, *refs):
    pages = refs[:N_PAGES]
    (win_ref, q_ref, kvn_ref, gl_ref, w1_ref, pe_ref, b1_ref, w2_ref, b2_ref,
     bc_ref, bs_ref, bw_ref, oks_ref, okw_ref, ex_ref, cov_ref, o_ref) = refs[N_PAGES:]
    rows_per_page = PAGE_SIZE // CMP_STRIDE

    ckv = []
    for s in range(2):
        def load(j, s=s):
            return jnp.concatenate(
                [pages[pg][0, 0, pl.ds(j * PAGE_SLOTS + s * NSA_KV_HEADS + g, rows_per_page,
                                       stride=CMP_STRIDE * PAGE_SLOTS), :]
                 for g in range(NSA_KV_HEADS) for pg in range(N_PAGES)], axis=0)

        ckv.append(_compress_rows(load, s, w1_ref, pe_ref, b1_ref, w2_ref, b2_ref))

    q = q_ref[...]
    kvn = kvn_ref[...]
    gs = jax.nn.sigmoid(gl_ref[...])
    rows = NSA_GROUP * DEC_SEQ
    lane_i = lax.broadcasted_iota(jnp.int32, (rows, LANE), 1)
    t_abs = PAST_LEN + lax.broadcasted_iota(jnp.int32, (SLC_PAD, LANE), 1)
    pad_keys = jnp.zeros((LANE - DEC_SEQ, LANE), F32)
    ok_slc = oks_ref[...] > 0.5
    ok_win = okw_ref[...] > 0.5

    def new_rows(branch, g):
        c0 = (branch * NSA_KV_HEADS + g) * LANE
        return kvn[:, c0:c0 + LANE]

    for g in range(NSA_KV_HEADS):
        qg = jnp.concatenate([q[:, (g * NSA_GROUP + j) * LANE:(g * NSA_GROUP + j + 1) * LANE]
                              for j in range(NSA_GROUP)], axis=0).astype(BF16)

        kc = ckv[0][g * N_CMP_PAD:(g + 1) * N_CMP_PAD].astype(BF16)
        vc = ckv[1][g * N_CMP_PAD:(g + 1) * N_CMP_PAD].astype(BF16)
        s = lax.dot_general(qg, kc, _NT, preferred_element_type=F32) * ATTN_SCALE + bc_ref[g]
        p = _softmax_rows(jnp.where(lane_i < N_CMP_PAD - 1, s, NEG_INF))
        o_cmp = jnp.dot(p.astype(BF16), vc, preferred_element_type=F32)

        psum = p[0:DEC_SEQ]
        for j in range(1, NSA_GROUP):
            psum = psum + p[j * DEC_SEQ:(j + 1) * DEC_SEQ]
        psum = jnp.concatenate([psum, jnp.zeros((LANE - DEC_SEQ, LANE), F32)], axis=0)
        sel = _select_blocks(psum, cov_ref[...], t_abs, S_KEYS // SLC_BLOCK - 1).astype(BF16)
        mask = jnp.dot(sel, ex_ref[...], preferred_element_type=F32)[:DEC_SEQ]
        mask = jnp.concatenate([mask] * NSA_GROUP, axis=0) > 0.5

        def page_rows(pg, slot):
            return pages[pg][0, 0, pl.ds(slot * NSA_KV_HEADS + g, PAGE_SIZE, stride=PAGE_SLOTS), :]

        def win_rows(slot):
            return win_ref[0, 0, pl.ds(slot * NSA_KV_HEADS + g, WINDOW, stride=WIN_SLOTS), :]

        kk = jnp.concatenate([page_rows(pg, 2) for pg in range(N_PAGES)]
                             + [new_rows(2, g), pad_keys], axis=0).astype(BF16)
        vv = jnp.concatenate([page_rows(pg, 3) for pg in range(N_PAGES)]
                             + [new_rows(3, g), pad_keys], axis=0).astype(BF16)
        s = lax.dot_general(qg, kk, _NT, preferred_element_type=F32) * ATTN_SCALE + bs_ref[g]
        p = _softmax_rows(jnp.where(mask & ok_slc, s, NEG_INF))
        o_slc = jnp.dot(p.astype(BF16), vv, preferred_element_type=F32)

        wk = jnp.concatenate([win_rows(0), new_rows(4, g), pad_keys], axis=0).astype(BF16)
        wv = jnp.concatenate([win_rows(1), new_rows(5, g), pad_keys], axis=0).astype(BF16)
        s = lax.dot_general(qg, wk, _NT, preferred_element_type=F32) * ATTN_SCALE + bw_ref[g]
        p = _softmax_rows(jnp.where(ok_win, s, NEG_INF))
        o_win = jnp.dot(p.astype(BF16), wv, preferred_element_type=F32)

        for j in range(NSA_GROUP):
            rs = slice(j * DEC_SEQ, (j + 1) * DEC_SEQ)
            c = g * LANE + j
            h = g * NSA_GROUP + j
            o_ref[:, h * LANE:(h + 1) * LANE] = (
                gs[:, c:c + 1] * o_cmp[rs] + gs[:, c + NSA_GROUP:c + NSA_GROUP + 1] * o_slc[rs]
                + gs[:, c + 2 * NSA_GROUP:c + 2 * NSA_GROUP + 1] * o_win[rs])


def _nsa_sample(page_table, cache, win_state, p_s, cw, tables, cov_t):
    cache = cache.reshape(cache.shape[0], cache.shape[1], PAGE_SIZE * PAGE_SLOTS, LANE)
    win_state = win_state.reshape(win_state.shape[0], win_state.shape[1], WINDOW * WIN_SLOTS, LANE)

    def page_spec(k):
        return pl.BlockSpec((1, 1, PAGE_SIZE * PAGE_SLOTS, LANE), lambda b, pt: (0, pt[b, k], 0, 0))

    consts = list(cw) + list(tables) + [cov_t]
    return pl.pallas_call(
        _nsa_sample_body,
        out_shape=SDS((N_SAMPLE, NSA_WIDTH), F32),
        grid_spec=pltpu.PrefetchScalarGridSpec(
            num_scalar_prefetch=1,
            grid=(DEC_BATCH,),
            in_specs=[page_spec(k) for k in range(N_PAGES)]
            + [pl.BlockSpec((1, 1, WINDOW * WIN_SLOTS, LANE), lambda b, pt: (0, b, 0, 0)),
               pl.BlockSpec((DEC_SEQ, NSA_WIDTH), lambda b, pt: (b, 0)),
               pl.BlockSpec((DEC_SEQ, KV_COLS), lambda b, pt: (b, COL_KV // KV_COLS)),
               pl.BlockSpec((DEC_SEQ, NSA_KV_HEADS * LANE), lambda b, pt: (b, COL_G // (NSA_KV_HEADS * LANE)))]
            + [_full_spec(a.shape) for a in consts],
            out_specs=pl.BlockSpec((DEC_SEQ, NSA_WIDTH), lambda b, pt: (b, 0))),
        compiler_params=_cparams(("arbitrary",)),
        name="nsa_sample",
    )(page_table, *([cache] * N_PAGES), win_state, p_s, p_s, p_s, *consts)


def _gmlp_body(u_ref, v_ref, g_ref, b_ref, w_ref, bs_ref, o_ref, ov_ref):
    for h in range(GM_HEADS):
        sl = slice(h * GM_HEAD_DIM, (h + 1) * GM_HEAD_DIM)
        vn = _normalize(jax.nn.gelu(v_ref[:, sl])) * g_ref[h] + b_ref[h]
        ov_ref[:, sl] = vn
        s = jnp.dot(w_ref[h], vn.astype(BF16), preferred_element_type=F32) + bs_ref[h]
        o_ref[:, sl] = jax.nn.gelu(u_ref[:, sl]) * s


def _gmlp(p, ln_g, ln_b, w_bf16, bs_rows):
    n = p.shape[0]
    row = pl.BlockSpec((GM_CHUNK, GM_WIDTH), lambda i: (i, 0))
    return pl.pallas_call(
        _gmlp_body,
        out_shape=(SDS((n, GM_WIDTH), F32), SDS((n, GM_WIDTH), F32)),
        grid=(n // GM_CHUNK,),
        in_specs=[pl.BlockSpec((GM_CHUNK, GM_WIDTH), lambda i: (i, COL_U // GM_WIDTH)),
                  pl.BlockSpec((GM_CHUNK, GM_WIDTH), lambda i: (i, COL_V // GM_WIDTH)),
                  _full_spec(ln_g.shape), _full_spec(ln_b.shape), _full_spec(w_bf16.shape),
                  _full_spec(bs_rows.shape)],
        out_specs=(row, row),
        compiler_params=_cparams(("arbitrary",)),
        name="gmlp",
    )(p, p, ln_g, ln_b, w_bf16, bs_rows)


def kernel(x_prompt, x_sample, cache_kv_pages, state_kv_window, page_table, c_prompt, c_sample, rel_bias_table,
           w_ada, b_ada, w_in, cmp_pe, cmp_w1, cmp_b1, cmp_w2, cmp_b2, gm_ln_g, gm_ln_b, gm_ws, gm_bs,
           g_norm_nsa, g_norm_gm, w_out, ln1_g, ln1_b, w_router, b_router, w_gu, b_gu, w_down, b_down,
           ln2_g, ln2_b):
    xp = x_prompt.reshape(N_PROMPT, D_MODEL)
    xs = x_sample.reshape(N_SAMPLE, D_MODEL)

    c_all = jnp.concatenate([c_prompt, c_sample, jnp.zeros((4, D_MODEL), F32)], axis=0)
    mod = _ada(c_all, w_ada[0], b_ada[0])
    mod_p = [mod[:BATCH, i * D_MODEL:(i + 1) * D_MODEL].reshape(BATCH, 1, D_MODEL) for i in range(6)]

    def per_token(v, tile):
        return jnp.repeat(v, DEC_SEQ, axis=0).reshape(N_SAMPLE // tile, tile, D_MODEL)

    mod_s = [mod[BATCH:BATCH + DEC_BATCH, i * D_MODEL:(i + 1) * D_MODEL] for i in range(6)]

    w = w_in[0]
    g0 = NSA_WIDTH + KV_COLS
    wg = w[:, g0:g0 + GATE_COLS].reshape(D_MODEL, N_BRANCH, NSA_KV_HEADS, NSA_GROUP)
    wg = jnp.transpose(wg, (0, 2, 1, 3)).reshape(D_MODEL, NSA_KV_HEADS, N_BRANCH * NSA_GROUP)
    wg = jnp.pad(wg, ((0, 0), (0, 0), (0, LANE - N_BRANCH * NSA_GROUP))).reshape(D_MODEL, NSA_KV_HEADS * LANE)
    w_r = jnp.concatenate([w[:, :NSA_WIDTH], w[:, g0 + GATE_COLS:], w[:, NSA_WIDTH:g0], wg], axis=1).astype(BF16)
    p_p = _inproj(xp, mod_p[1], mod_p[0], w_r, SEQ, 512)
    p_s = _inproj(xs, per_token(mod_s[1], 256), per_token(mod_s[0], 256), w_r, None, 256)

    tbl = rel_bias_table[jnp.asarray(_BUCKET_LUT)].T
    cw = (cmp_w1[0].reshape(2, 2, CMP_STRIDE // 2, 2 * HEAD_DIM, HEAD_DIM).astype(BF16), cmp_pe[0],
          cmp_b1[0].reshape(2, 1, HEAD_DIM), cmp_w2[0].astype(BF16), cmp_b2[0].reshape(2, 1, HEAD_DIM))
    ln_g = gm_ln_g[0].reshape(GM_HEADS, 1, GM_HEAD_DIM)
    ln_b = gm_ln_b[0].reshape(GM_HEADS, 1, GM_HEAD_DIM)
    tril = jnp.tril(jnp.ones((GM_CHUNK, GM_CHUNK), bool))

    near, b_cmp, expand = _prompt_tables(tbl)
    ckv_p = _compress_prompt(p_p, cw)
    o_nsa_p = _nsa_prompt(p_p, ckv_p, near, b_cmp, expand, _cover_t(SEQ // SLC_BLOCK))
    w_sp = jnp.where(tril, gm_ws[0], 0).astype(BF16)
    bs_p = jnp.broadcast_to(gm_bs[0][:, :, None], (GM_HEADS, GM_CHUNK, GM_HEAD_DIM))
    gm_p, _ = _gmlp(p_p, ln_g, ln_b, w_sp, bs_p)
    p3 = p_p.reshape(BATCH, SEQ, IN_COLS_PAD)
    half = KV_COLS // N_BRANCH * 2
    prompt_kv_rows = p3[:, :, COL_KV:COL_KV + half].reshape(1, BATCH, SEQ, 4, NSA_KV_HEADS, HEAD_DIM)
    prompt_kv_window = p3[:, SEQ - WINDOW:, COL_KV + half:COL_G].reshape(1, BATCH, WINDOW, 2, NSA_KV_HEADS, HEAD_DIM)

    o_nsa_s = _nsa_sample(page_table, cache_kv_pages, state_kv_window, p_s, cw, _sample_tables(tbl),
                          _cover_t(S_KEYS // SLC_BLOCK - 1))
    seqs = GM_CHUNK // DEC_SEQ
    w_small = jnp.where(tril[:DEC_SEQ, :DEC_SEQ], gm_ws[0][:, :DEC_SEQ, :DEC_SEQ], 0)
    w_ss = jnp.einsum('ab,hij->haibj', jnp.eye(seqs, dtype=F32), w_small).reshape(GM_HEADS, GM_CHUNK, GM_CHUNK)
    bs_s = jnp.broadcast_to(jnp.tile(gm_bs[0][:, :DEC_SEQ], (1, seqs))[:, :, None],
                            (GM_HEADS, GM_CHUNK, GM_HEAD_DIM))
    gm_s, gmv_s = _gmlp(p_s, ln_g, ln_b, w_ss.astype(BF16), bs_s)
    s3 = p_s.reshape(DEC_BATCH, DEC_SEQ, IN_COLS_PAD)
    sample_kv_rows = s3[:, :, COL_KV:COL_KV + half].reshape(1, DEC_BATCH, DEC_SEQ, 4, NSA_KV_HEADS, HEAD_DIM)
    win_new = s3[:, :, COL_KV + half:COL_G].reshape(DEC_BATCH, DEC_SEQ, 2, NSA_KV_HEADS, HEAD_DIM)
    sample_kv_window = jnp.concatenate([state_kv_window[0][:, DEC_SEQ:], win_new], axis=1)[None]
    sample_gmlp_v = gmv_s.reshape(1, DEC_BATCH, DEC_SEQ, GM_HEADS, GM_HEAD_DIM)

    w_ob = w_out[0].astype(BF16)
    mix_p = _matmul(_merge(o_nsa_p, gm_p, g_norm_nsa[0], g_norm_gm[0]), w_ob)
    mix_s = _matmul(_merge(o_nsa_s, gm_s, g_norm_nsa[0], g_norm_gm[0]), w_ob)
    wr = jnp.pad(w_router[0], ((0, 0), (0, 128 - N_EXPERTS))).astype(BF16)
    br = jnp.pad(b_router[0], (0, 128 - N_EXPERTS)).reshape(1, 128)
    y1_p, h_p, lg_p = _post1(xp, mix_p, mod_p[2], mod_p[4], mod_p[3], ln1_g[0], ln1_b[0], wr, br, SEQ)
    y1_s, h_s, lg_s = _post1(xs, mix_s, per_token(mod_s[2], POST_T), per_token(mod_s[4], POST_T),
                             per_token(mod_s[3], POST_T), ln1_g[0], ln1_b[0], wr, br, None)

    h = jnp.concatenate([h_p, h_s], axis=0)
    logits = jnp.concatenate([lg_p, lg_s], axis=0)[:, :N_EXPERTS]
    gates, row_tok, pos, route = _route(logits)
    xg = _moe_gather(row_tok, route[1], h)
    hact = _moe_gate_up(route, xg, w_gu[0], b_gu[0])
    yb = _moe_down(route, hact, w_down[0], b_down[0])
    y_p = _moe_combine(pos[:N_PROMPT * TOP_K], yb, y1_p, gates[:N_PROMPT], mod_p[5], ln2_g[0], ln2_b[0], SEQ, 0)
    y_s = _moe_combine(pos[N_PROMPT * TOP_K:], yb, y1_s, gates[N_PROMPT:], per_token(mod_s[5], COMB_T),
                       ln2_g[0], ln2_b[0], None, 0)
    return (y_p.reshape(BATCH, SEQ, D_MODEL), y_s.reshape(DEC_BATCH, DEC_SEQ, D_MODEL),
            prompt_kv_rows, prompt_kv_window, sample_kv_rows, sample_kv_window, sample_gmlp_v)
```

```python
import functools
import math

import numpy as np
import jax
import jax.numpy as jnp
from jax import lax
from jax.experimental import pallas as pl
from jax.experimental.pallas import tpu as pltpu

F32 = jnp.float32
BF16 = jnp.bfloat16
SDS = jax.ShapeDtypeStruct

D_MODEL = 4096
BATCH = 4
SEQ = 2048
DEC_BATCH = 128
DEC_SEQ = 8
PAST_LEN = 2048
PAGE_SIZE = 128
N_PAGES = PAST_LEN // PAGE_SIZE
HEAD_DIM = 128
NSA_WIDTH = D_MODEL // 2
NSA_HEADS = NSA_WIDTH // HEAD_DIM
NSA_KV_HEADS = 4
NSA_GROUP = NSA_HEADS // NSA_KV_HEADS
N_BRANCH = 3
CMP_BLOCK = 32
CMP_STRIDE = 16
SLC_BLOCK = 64
SLC_TOPN = 16
SLC_LOCAL = 2
WINDOW = 512
FORCE_SCORE = 1e4
ATTN_SCALE = HEAD_DIM ** -0.5
NEG_INF = -1e30
GM_WIDTH = D_MODEL - NSA_WIDTH
GM_HEAD_DIM = 128
GM_HEADS = GM_WIDTH // GM_HEAD_DIM
GM_CHUNK = 128
KV_COLS = N_BRANCH * 2 * NSA_KV_HEADS * HEAD_DIM
GATE_COLS = N_BRANCH * NSA_HEADS
NUM_BUCKETS = 32
MAX_EXACT = NUM_BUCKETS // 2
MAX_DISTANCE = 128
N_EXPERTS = 32
TOP_K = 4
D_FF = D_MODEL
SWIGLU_LIMIT = 7.0
SWIGLU_ALPHA = 1.702
DN_ALPHA = 2.0 ** 0.25
LN_EPS = 1e-5

N_PROMPT = BATCH * SEQ
N_SAMPLE = DEC_BATCH * DEC_SEQ
N_TOK = N_PROMPT + N_SAMPLE

COL_Q = 0
COL_U = NSA_WIDTH
COL_V = COL_U + GM_WIDTH
COL_KV = COL_V + GM_WIDTH
COL_G = COL_KV + KV_COLS
LANE = 128
IN_COLS_PAD = COL_G + NSA_KV_HEADS * LANE
QBLK = 128
N_QBLK = SEQ // QBLK
N_CMP_PAD = 128
SLC_PAD = 40
S_KEYS = PAST_LEN + LANE
W_KEYS = WINDOW + LANE
PAGE_SLOTS = 4 * NSA_KV_HEADS
WIN_SLOTS = 2 * NSA_KV_HEADS

VMEM_LIMIT = 56 * 1024 * 1024


def _cparams(sem):
    return pltpu.CompilerParams(dimension_semantics=sem, vmem_limit_bytes=VMEM_LIMIT)


def _normalize(x):
    xc = x - jnp.mean(x, axis=-1, keepdims=True)
    return xc * lax.rsqrt(jnp.mean(xc * xc, axis=-1, keepdims=True) + LN_EPS)


def _ada_body(c_ref, w_ref, b_ref, o_ref):
    c = c_ref[...]
    a = (c * jax.nn.sigmoid(c)).astype(BF16)
    o_ref[...] = jnp.dot(a, w_ref[...].astype(BF16), preferred_element_type=F32) + b_ref[...]


def _ada(c_all, w_ada, b_ada):
    rows = c_all.shape[0]
    tn = 512
    return pl.pallas_call(
        _ada_body,
        out_shape=SDS((rows, 6 * D_MODEL), F32),
        grid=(6 * D_MODEL // tn,),
        in_specs=[pl.BlockSpec((rows, D_MODEL), lambda j: (0, 0)),
                  pl.BlockSpec((D_MODEL, tn), lambda j: (0, j)),
                  pl.BlockSpec((1, tn), lambda j: (0, j))],
        out_specs=pl.BlockSpec((rows, tn), lambda j: (0, j)),
        compiler_params=_cparams(("arbitrary",)),
        name="ada",
    )(c_all, w_ada, b_ada.reshape(1, -1))


def _inproj_body(x_ref, sc_ref, sh_ref, w_ref, o_ref, m_ref):
    @pl.when(pl.program_id(1) == 0)
    def _():
        rows = x_ref.shape[0]
        per_token = sc_ref.shape[1] != 1
        for r0 in range(0, rows, 128):
            rs = slice(r0, r0 + 128)
            sc = sc_ref[0, rs] if per_token else sc_ref[0]
            sh = sh_ref[0, rs] if per_token else sh_ref[0]
            m_ref[rs] = (_normalize(x_ref[rs]) * (1.0 + sc) + sh).astype(BF16)

    o_ref[...] = jnp.dot(m_ref[...], w_ref[...], preferred_element_type=F32)


def _mod_spec(tile_rows, rows_per_group):
    if rows_per_group is None:
        return pl.BlockSpec((1, tile_rows, D_MODEL), lambda i, *_: (i, 0, 0))
    per = rows_per_group // tile_rows
    return pl.BlockSpec((1, 1, D_MODEL), lambda i, *_: (i // per, 0, 0))


def _inproj(x, sc, sh, w_bf16, rows_per_group, tm):
    n = x.shape[0]
    tn = 512
    ncols = w_bf16.shape[1]
    return pl.pallas_call(
        _inproj_body,
        out_shape=SDS((n, ncols), F32),
        grid=(n // tm, ncols // tn),
        in_specs=[pl.BlockSpec((tm, D_MODEL), lambda i, j: (i, 0)),
                  _mod_spec(tm, rows_per_group), _mod_spec(tm, rows_per_group),
                  pl.BlockSpec((D_MODEL, tn), lambda i, j: (0, j))],
        out_specs=pl.BlockSpec((tm, tn), lambda i, j: (i, j)),
        scratch_shapes=[pltpu.VMEM((tm, D_MODEL), BF16)],
        compiler_params=_cparams(("arbitrary", "arbitrary")),
        name="inproj",
    )(x, sc, sh, w_bf16)


def _merge_body(o_ref, gm_ref, gn_ref, gg_ref, cat_ref):
    o = o_ref[...]
    gm = gm_ref[...]
    on = o * lax.rsqrt(jnp.mean(o * o, axis=-1, keepdims=True) + LN_EPS) * gn_ref[...]
    gn = gm * lax.rsqrt(jnp.mean(gm * gm, axis=-1, keepdims=True) + LN_EPS) * gg_ref[...]
    cat_ref[:, :NSA_WIDTH] = on.astype(BF16)
    cat_ref[:, NSA_WIDTH:] = gn.astype(BF16)


def _merge(o_nsa, gm_out, g_nsa, g_gm):
    n = o_nsa.shape[0]
    tr = 512
    return pl.pallas_call(
        _merge_body,
        out_shape=SDS((n, D_MODEL), BF16),
        grid=(n // tr,),
        in_specs=[pl.BlockSpec((tr, NSA_WIDTH), lambda i: (i, 0)),
                  pl.BlockSpec((tr, GM_WIDTH), lambda i: (i, 0)),
                  pl.BlockSpec((1, NSA_WIDTH), lambda i: (0, 0)),
                  pl.BlockSpec((1, GM_WIDTH), lambda i: (0, 0))],
        out_specs=pl.BlockSpec((tr, D_MODEL), lambda i: (i, 0)),
        compiler_params=_cparams(("arbitrary",)),
        name="merge",
    )(o_nsa, gm_out, g_nsa.reshape(1, -1), g_gm.reshape(1, -1))


def _mm_body(x_ref, w_ref, o_ref):
    o_ref[...] = jnp.dot(x_ref[...], w_ref[...], preferred_element_type=F32)


def _matmul(x_bf16, w_bf16, tm=512, tn=1024):
    n, k = x_bf16.shape
    ncols = w_bf16.shape[1]
    return pl.pallas_call(
        _mm_body,
        out_shape=SDS((n, ncols), F32),
        grid=(n // tm, ncols // tn),
        in_specs=[pl.BlockSpec((tm, k), lambda i, j: (i, 0)),
                  pl.BlockSpec((k, tn), lambda i, j: (0, j))],
        out_specs=pl.BlockSpec((tm, tn), lambda i, j: (i, j)),
        compiler_params=_cparams(("arbitrary", "arbitrary")),
        name="outproj",
    )(x_bf16, w_bf16)


POST_T = 128


def _post1_body(x_ref, mix_ref, gt_ref, sc_ref, sh_ref, g_ref, b_ref, wr_ref, br_ref,
                y_ref, h_ref, idx_ref, gate_ref):
    z = DN_ALPHA * x_ref[...] + gt_ref[0] * mix_ref[...]
    y = _normalize(z) * g_ref[...] + b_ref[...]
    y_ref[...] = y
    h = _normalize(y) * (1.0 + sc_ref[0]) + sh_ref[0]
    h_ref[...] = h
    logits = jnp.dot(h.astype(BF16), wr_ref[...], preferred_element_type=F32) + br_ref[...]
    lane = lax.broadcasted_iota(jnp.int32, logits.shape, 1).astype(F32)
    lg = jnp.where(lane < N_EXPERTS, logits, -jnp.inf)
    idx_out = jnp.zeros(logits.shape, F32)
    val_out = jnp.zeros(logits.shape, F32)
    top = None
    den = None
    for k in range(TOP_K):
        m = jnp.max(lg, axis=-1, keepdims=True)
        pick = jnp.min(jnp.where(lg == m, lane, float(LANE)), axis=-1, keepdims=True)
        lg = jnp.where(lane == pick, -jnp.inf, lg)
        top = m if top is None else top
        e = jnp.exp(m - top)
        den = e if den is None else den + e
        idx_out = jnp.where(lane == k, pick, idx_out)
        val_out = jnp.where(lane == k, e, val_out)
    idx_ref[...] = idx_out.astype(jnp.int32)
    gate_ref[...] = val_out / den


def _post1(x, mix, gt, sc, sh, ln_g, ln_b, wr_bf16, br, rows_per_group):
    n = x.shape[0]
    tr = POST_T
    row = pl.BlockSpec((tr, D_MODEL), lambda i: (i, 0))
    vec = pl.BlockSpec((1, D_MODEL), lambda i: (0, 0))
    ms = _mod_spec(tr, rows_per_group)
    return pl.pallas_call(
        _post1_body,
        out_shape=(SDS((n, D_MODEL), F32), SDS((n, D_MODEL), F32), SDS((n, LANE), jnp.int32), SDS((n, LANE), F32)),
        grid=(n // tr,),
        in_specs=[row, row, ms, ms, ms, vec, vec,
                  pl.BlockSpec((D_MODEL, LANE), lambda i: (0, 0)),
                  pl.BlockSpec((1, LANE), lambda i: (0, 0))],
        out_specs=(row, row, pl.BlockSpec((tr, LANE), lambda i: (i, 0)), pl.BlockSpec((tr, LANE), lambda i: (i, 0))),
        compiler_params=_cparams(("arbitrary",)),
        name="post1",
    )(x, mix, gt, sc, sh, ln_g.reshape(1, -1), ln_b.reshape(1, -1), wr_bf16, br)


MOE_TM = 512
MOE_TILES = -(-(N_TOK * TOP_K + N_EXPERTS * (MOE_TM - 1)) // MOE_TM)
MOE_ROWS_PAD = MOE_TILES * MOE_TM


def _gather_body(tok_ref, nu_ref, h_ref, o_ref, buf_ref, sem):
    i = pl.program_id(0)

    @pl.when(i < nu_ref[0])
    def _():
        base = i * MOE_TM

        def row_copy(r, tok):
            return pltpu.make_async_copy(h_ref.at[pl.ds(tok, 1), :], buf_ref.at[pl.ds(r, 1), :], sem)

        def issue(r, c):
            row_copy(r, tok_ref[base + r]).start()
            return c

        lax.fori_loop(0, MOE_TM, issue, 0, unroll=8)

        def drain(r, c):
            row_copy(r, 0).wait()
            return c

        lax.fori_loop(0, MOE_TM, drain, 0, unroll=8)
        o_ref[...] = buf_ref[...].astype(BF16)

    @pl.when(i >= nu_ref[0])
    def _():
        o_ref[...] = jnp.zeros_like(o_ref)


def _moe_gather(row_tok, n_used, h):
    return pl.pallas_call(
        _gather_body,
        out_shape=SDS((MOE_ROWS_PAD, D_MODEL), BF16),
        grid_spec=pltpu.PrefetchScalarGridSpec(
            num_scalar_prefetch=2,
            grid=(MOE_TILES,),
            in_specs=[pl.BlockSpec(memory_space=pl.ANY)],
            out_specs=pl.BlockSpec((MOE_TM, D_MODEL), lambda i, tok, nu: (i, 0)),
            scratch_shapes=[pltpu.VMEM((MOE_TM, D_MODEL), F32), pltpu.SemaphoreType.DMA(())]),
        compiler_params=_cparams(("arbitrary",)),
        name="moe_gather",
    )(row_tok, n_used, h)


def _first_of_expert(m, te_ref):
    return (m == 0) | (te_ref[m] != te_ref[jnp.maximum(m - 1, 0)])


def _expert_weights(te_ref, rid_ref, rexp_ref, nr_ref, w_hbm, wbuf, sem, col_offsets, tn, casted):
    n = pl.program_id(0)
    m = pl.program_id(1)

    def copies(e, nn, slot):
        return [pltpu.make_async_copy(w_hbm.at[e, :, pl.ds(pl.multiple_of(off + nn * tn, tn), tn)],
                                      wbuf.at[slot, i], sem.at[slot]) for i, off in enumerate(col_offsets)]

    @pl.when(_first_of_expert(m, te_ref))
    def _():
        n_runs = nr_ref[0]
        k = rid_ref[m]
        c = n * n_runs + k
        slot = c % 2

        @pl.when(c == 0)
        def _():
            for cp in copies(te_ref[0], 0, 0):
                cp.start()

        wrap = k + 1 >= n_runs
        k2 = jnp.where(wrap, 0, k + 1)
        n2 = jnp.where(wrap, n + 1, n)

        @pl.when(n2 < pl.num_programs(0))
        def _():
            for cp in copies(rexp_ref[k2], n2, 1 - slot):
                cp.start()

        for cp in copies(0, 0, slot):
            cp.wait()
        for i, dst in enumerate(casted):
            dst[...] = wbuf[slot, i].astype(BF16)


def _gu_body(te_ref, nu_ref, rid_ref, rexp_ref, nr_ref, x_ref, w_hbm, bg_ref, bl_ref, o_ref,
             wbuf, wgb_ref, wlb_ref, sem):
    m = pl.program_id(1)

    @pl.when(m < nu_ref[0])
    def _():
        _expert_weights(te_ref, rid_ref, rexp_ref, nr_ref, w_hbm, wbuf, sem, (0, D_FF), GU_TN,
                        (wgb_ref, wlb_ref))
        x = x_ref[...]
        gl = jnp.dot(x, wgb_ref[...], preferred_element_type=F32) + bg_ref[...]
        lin = jnp.dot(x, wlb_ref[...], preferred_element_type=F32) + bl_ref[...]
        gl = jnp.minimum(gl, SWIGLU_LIMIT)
        lin = jnp.clip(lin, -SWIGLU_LIMIT, SWIGLU_LIMIT)
        o_ref[...] = ((lin + 1.0) * (gl * jax.nn.sigmoid(SWIGLU_ALPHA * gl))).astype(BF16)

    @pl.when(m >= nu_ref[0])
    def _():
        o_ref[...] = jnp.zeros_like(o_ref)


GU_TN = 512
DOWN_TN = 1024


def _tile_idx(m, nu):
    return jnp.minimum(m, nu[0] - 1)


def _moe_gate_up(route, xs, w_gu, b_gu):
    tn = GU_TN
    nt = D_FF // tn
    bias = b_gu.reshape(N_EXPERTS, 1, -1)
    return pl.pallas_call(
        _gu_body,
        out_shape=SDS((MOE_ROWS_PAD, D_FF), BF16),
        grid_spec=pltpu.PrefetchScalarGridSpec(
            num_scalar_prefetch=5,
            grid=(nt, MOE_TILES),
            in_specs=[pl.BlockSpec((MOE_TM, D_MODEL), lambda n, m, te, nu, *_: (_tile_idx(m, nu), 0)),
                      pl.BlockSpec(memory_space=pl.ANY),
                      pl.BlockSpec((None, 1, tn), lambda n, m, te, nu, *_: (te[_tile_idx(m, nu)], 0, n)),
                      pl.BlockSpec((None, 1, tn), lambda n, m, te, nu, *_: (te[_tile_idx(m, nu)], 0, n + nt))],
            out_specs=pl.BlockSpec((MOE_TM, tn), lambda n, m, *_: (m, n)),
            scratch_shapes=[pltpu.VMEM((2, 2, D_MODEL, tn), F32), pltpu.VMEM((D_MODEL, tn), BF16),
                            pltpu.VMEM((D_MODEL, tn), BF16), pltpu.SemaphoreType.DMA((2,))]),
        compiler_params=_cparams(("arbitrary", "arbitrary")),
        name="moe_gate_up",
    )(*route, xs, w_gu, bias, bias)


def _down_body(te_ref, nu_ref, rid_ref, rexp_ref, nr_ref, x_ref, w_hbm, b_ref, o_ref, wbuf, wb_ref, sem):
    m = pl.program_id(1)

    @pl.when(m < nu_ref[0])
    def _():
        _expert_weights(te_ref, rid_ref, rexp_ref, nr_ref, w_hbm, wbuf, sem, (0,), DOWN_TN, (wb_ref,))
        o_ref[...] = jnp.dot(x_ref[...], wb_ref[...], preferred_element_type=F32) + b_ref[...]

    @pl.when(m >= nu_ref[0])
    def _():
        o_ref[...] = jnp.zeros_like(o_ref)


def _moe_down(route, hact, w_down, b_down):
    tn = DOWN_TN
    nt = D_MODEL // tn
    return pl.pallas_call(
        _down_body,
        out_shape=SDS((MOE_ROWS_PAD, D_MODEL), F32),
        grid_spec=pltpu.PrefetchScalarGridSpec(
            num_scalar_prefetch=5,
            grid=(nt, MOE_TILES),
            in_specs=[pl.BlockSpec((MOE_TM, D_FF), lambda n, m, te, nu, *_: (_tile_idx(m, nu), 0)),
                      pl.BlockSpec(memory_space=pl.ANY),
                      pl.BlockSpec((None, 1, tn), lambda n, m, te, nu, *_: (te[_tile_idx(m, nu)], 0, n))],
            out_specs=pl.BlockSpec((MOE_TM, tn), lambda n, m, *_: (m, n)),
            scratch_shapes=[pltpu.VMEM((2, 1, D_FF, tn), F32), pltpu.VMEM((D_FF, tn), BF16),
                            pltpu.SemaphoreType.DMA((2,))]),
        compiler_params=_cparams(("arbitrary", "arbitrary")),
        name="moe_down",
    )(*route, hact, w_down, b_down.reshape(N_EXPERTS, 1, -1))


COMB_T = 128


def _combine_body(pos_ref, yb_ref, y1_ref, gates_ref, gt_ref, g_ref, b_ref, o_ref, buf_ref, sem):
    i = pl.program_id(0)
    base = i * COMB_T

    def row_copy(k, r, src):
        return pltpu.make_async_copy(yb_ref.at[pl.ds(src, 1), :], buf_ref.at[k, pl.ds(r, 1), :], sem)

    def issue(r, c):
        for k in range(TOP_K):
            row_copy(k, r, pos_ref[(base + r) * TOP_K + k]).start()
        return c

    lax.fori_loop(0, COMB_T, issue, 0, unroll=4)

    def drain(r, c):
        for k in range(TOP_K):
            row_copy(k, r, 0).wait()
        return c

    lax.fori_loop(0, COMB_T, drain, 0, unroll=4)
    gates = gates_ref[...]
    f = buf_ref[0] * gates[:, 0:1]
    for k in range(1, TOP_K):
        f = f + buf_ref[k] * gates[:, k:k + 1]
    z = DN_ALPHA * y1_ref[...] + gt_ref[0] * f
    o_ref[...] = _normalize(z) * g_ref[...] + b_ref[...]


def _moe_combine(pos, yb, y1, gates, gt, ln_g, ln_b, rows_per_group, tok_offset):
    n = y1.shape[0]
    row = pl.BlockSpec((COMB_T, D_MODEL), lambda i, pos: (i, 0))
    vec = pl.BlockSpec((1, D_MODEL), lambda i, pos: (0, 0))
    return pl.pallas_call(
        _combine_body,
        out_shape=SDS((n, D_MODEL), F32),
        grid_spec=pltpu.PrefetchScalarGridSpec(
            num_scalar_prefetch=1,
            grid=(n // COMB_T,),
            in_specs=[pl.BlockSpec(memory_space=pl.ANY), row,
                      pl.BlockSpec((COMB_T, TOP_K), lambda i, pos: (i, 0)),
                      _mod_spec(COMB_T, rows_per_group), vec, vec],
            out_specs=row,
            scratch_shapes=[pltpu.VMEM((TOP_K, COMB_T, D_MODEL), F32), pltpu.SemaphoreType.DMA(())]),
        compiler_params=_cparams(("arbitrary",)),
        name="moe_combine",
    )(pos, yb, y1, gates, gt, ln_g.reshape(1, -1), ln_b.reshape(1, -1))


def _route(top_idx):
    n_assign = N_TOK * TOP_K
    flat_e = top_idx.reshape(-1)
    order = jnp.argsort(flat_e)
    e_sorted = flat_e[order]
    counts = jnp.bincount(flat_e, length=N_EXPERTS)
    padded = (counts + MOE_TM - 1) // MOE_TM * MOE_TM
    pad_end = jnp.cumsum(padded)
    pad_start = pad_end - padded
    start = jnp.cumsum(counts) - counts
    dest = (pad_start[e_sorted] + jnp.arange(n_assign) - start[e_sorted]).astype(jnp.int32)
    row_tok = jnp.zeros((MOE_ROWS_PAD,), jnp.int32).at[dest].set((order // TOP_K).astype(jnp.int32))
    pos = jnp.zeros((n_assign,), jnp.int32).at[order].set(dest)
    tile_e = jnp.minimum(jnp.searchsorted(pad_end, jnp.arange(MOE_TILES) * MOE_TM, side='right'),
                         N_EXPERTS - 1).astype(jnp.int32)
    n_used = (pad_end[-1] // MOE_TM).astype(jnp.int32).reshape(1)
    tiles = jnp.arange(MOE_TILES)
    first = ((tiles == 0) | (tile_e != jnp.roll(tile_e, 1))) & (tiles < n_used[0])
    run_id = (jnp.cumsum(first) - 1).astype(jnp.int32)
    run_expert = jnp.zeros((N_EXPERTS,), jnp.int32).at[jnp.where(first, run_id, N_EXPERTS)].set(tile_e, mode='drop')
    n_runs = jnp.sum(first).astype(jnp.int32).reshape(1)
    return row_tok, pos, (tile_e, n_used, run_id, run_expert, n_runs)


def _bucket_lut():
    n = np.arange(0, MAX_DISTANCE + 1)
    nf = np.maximum(n, 1).astype(np.float32)
    large = MAX_EXACT + (np.log(nf / MAX_EXACT) / math.log(MAX_DISTANCE / MAX_EXACT)
                         * (NUM_BUCKETS - MAX_EXACT)).astype(np.int32)
    return np.where(n < MAX_EXACT, n, np.minimum(large, NUM_BUCKETS - 1)).astype(np.int32)


_BUCKET_LUT = _bucket_lut()


def _bias_at(tbl, dist):
    return tbl[:, jnp.clip(dist, 0, MAX_DISTANCE)]


def _cover_t(n_slc):
    s0 = np.arange(LANE)[:, None] * SLC_BLOCK
    c0 = np.arange(LANE)[None, :] * CMP_STRIDE
    cov = (c0 < s0 + SLC_BLOCK) & (c0 + CMP_BLOCK > s0) & (np.arange(LANE)[None, :] < N_CMP_PAD - 1)
    return jnp.asarray(cov & (np.arange(LANE)[:, None] < n_slc), BF16)


def _prompt_tables(tbl):
    i = np.arange(QBLK)[:, None]
    k = np.arange(QBLK)[None, :]
    near = jnp.stack([_bias_at(tbl, i - k), _bias_at(tbl, QBLK + i - k), _bias_at(tbl, 2 * QBLK + i - k)])
    near = near.reshape(3, NSA_KV_HEADS, NSA_GROUP, QBLK, QBLK).transpose(0, 1, 4, 2, 3)
    near = near.reshape(3, NSA_KV_HEADS, QBLK, NSA_GROUP * QBLK)
    last = (N_CMP_PAD - 1) * CMP_STRIDE + CMP_BLOCK - 1
    ext = jnp.concatenate([jnp.broadcast_to(tbl[:, :1], (NSA_HEADS, last)), tbl,
                           jnp.broadcast_to(tbl[:, -1:], (NSA_HEADS, SEQ - MAX_DISTANCE - 1))], axis=1)
    cols = [lax.slice_in_dim(ext, last - (c * CMP_STRIDE + CMP_BLOCK - 1),
                             last - (c * CMP_STRIDE + CMP_BLOCK - 1) + SEQ, axis=1) for c in range(N_CMP_PAD)]
    b_cmp = jnp.transpose(jnp.stack(cols, axis=1), (0, 2, 1))
    key = np.arange(SEQ).reshape(N_QBLK, 1, QBLK)
    expand = jnp.asarray((key // SLC_BLOCK).transpose(0, 2, 1) == np.arange(LANE)[None, None, :],
                         BF16)
    return near, b_cmp, expand


def _sample_tables(tbl):
    t = np.arange(DEC_SEQ)[:, None]
    qpos = PAST_LEN + t

    def rows(x):
        return x.reshape(NSA_KV_HEADS, NSA_GROUP * DEC_SEQ, x.shape[-1])

    def tile(m):
        return jnp.asarray(np.tile(m, (NSA_GROUP, 1)), F32)

    c = np.arange(N_CMP_PAD)[None, :]
    b_cmp = rows(_bias_at(tbl, qpos - (c * CMP_STRIDE + CMP_BLOCK - 1)))
    ks = np.arange(S_KEYS)[None, :]
    b_slc = rows(_bias_at(tbl, qpos - ks))
    ok_slc = tile((ks <= qpos) & (ks < PAST_LEN + DEC_SEQ))
    kw = np.arange(W_KEYS)[None, :]
    d_win = WINDOW + t - kw
    b_win = rows(_bias_at(tbl, d_win))
    ok_win = tile((d_win >= 0) & (d_win < WINDOW) & (kw < WINDOW + DEC_SEQ))
    expand = jnp.asarray(np.arange(S_KEYS)[None, :] // SLC_BLOCK == np.arange(LANE)[:, None], BF16)
    return b_cmp, b_slc, b_win, ok_slc, ok_win, expand


_NT = (((1,), (1,)), ((), ()))


def _compress_rows(load, s, w1_ref, pe_ref, b1_ref, w2_ref, b2_ref):
    acc = [None, None]
    for jp in range(CMP_STRIDE // 2):
        xa, xb = load(2 * jp), load(2 * jp + 1)
        for r in range(2):
            ia = r * CMP_STRIDE + 2 * jp
            lhs = jnp.concatenate([xa + pe_ref[s, pl.ds(ia, 1), :], xb + pe_ref[s, pl.ds(ia + 1, 1), :]],
                                  axis=1).astype(BF16)
            d = jnp.dot(lhs, w1_ref[s, r, jp], preferred_element_type=F32)
            acc[r] = d if acc[r] is None else acc[r] + d
    h = b1_ref[s] + acc[0] + pltpu.roll(acc[1], NSA_KV_HEADS * N_CMP_PAD - 1, 0)
    return jnp.dot(jax.nn.gelu(h).astype(BF16), w2_ref[s], preferred_element_type=F32) + b2_ref[s]


def _cmp_prompt_body(*refs):
    n_x = 2 * NSA_KV_HEADS
    x_refs = refs[:n_x]
    w1_ref, pe_ref, b1_ref, w2_ref, b2_ref, o_ref = refs[n_x:]
    for s in range(2):
        def load(j, s=s):
            return jnp.concatenate(
                [x_refs[s * NSA_KV_HEADS + g][pl.ds(j, N_CMP_PAD, stride=CMP_STRIDE), :]
                 for g in range(NSA_KV_HEADS)], axis=0)

        out = _compress_rows(load, s, w1_ref, pe_ref, b1_ref, w2_ref, b2_ref)
        for g in range(NSA_KV_HEADS):
            o_ref[0, s, g] = out[g * N_CMP_PAD:(g + 1) * N_CMP_PAD]


def _full_spec(shape):
    nd = len(shape)
    return pl.BlockSpec(shape, lambda *_: (0,) * nd)


def _compress_prompt(p_p, cw):
    return pl.pallas_call(
        _cmp_prompt_body,
        out_shape=SDS((BATCH, 2, NSA_KV_HEADS, N_CMP_PAD, LANE), F32),
        grid=(BATCH,),
        in_specs=[pl.BlockSpec((SEQ, LANE), lambda b, c=COL_KV // LANE + sg: (b, c))
                  for sg in range(2 * NSA_KV_HEADS)]
        + [_full_spec(a.shape) for a in cw],
        out_specs=pl.BlockSpec((1, 2, NSA_KV_HEADS, N_CMP_PAD, LANE), lambda b: (b, 0, 0, 0, 0)),
        compiler_params=_cparams(("arbitrary",)),
        name="compress_prompt",
    )(*([p_p] * (2 * NSA_KV_HEADS)), *cw)


def _select_blocks(psum, cov_t, t_abs, n_slc, transpose=True):
    hi = psum.astype(BF16)
    lo = (psum - hi.astype(F32)).astype(BF16)
    imp = (lax.dot_general(cov_t, hi, _NT, preferred_element_type=F32)
           + lax.dot_general(cov_t, lo, _NT, preferred_element_type=F32))[:SLC_PAD]
    blk = lax.broadcasted_iota(jnp.int32, (SLC_PAD, LANE), 0)
    cur = jnp.right_shift(t_abs, SLC_BLOCK.bit_length() - 1)
    forced = (blk == 0) | ((cur - blk >= 0) & (cur - blk < SLC_LOCAL))
    score = jnp.where(blk <= cur, imp + jnp.where(forced, FORCE_SCORE, 0.0), -1.0)
    score = jnp.where(blk < n_slc, score, -2.0)
    ahead = jnp.zeros((SLC_PAD, LANE), F32)
    for s2 in range(n_slc):
        row = score[s2:s2 + 1, :]
        ahead = ahead + jnp.where((row > score) | ((row == score) & (blk > s2)), 1.0, 0.0)
    sel_t = jnp.where(ahead < SLC_TOPN, 1.0, 0.0)
    sel_t = jnp.concatenate([sel_t, jnp.zeros((LANE - SLC_PAD, LANE), F32)], axis=0)
    return sel_t.T if transpose else sel_t


def _nsa_prompt_body(q_ref, ckv_ref, ks_ref, vs_ref, kw_ref, vw_ref, gl_ref, bc_ref, b3_ref, ex_ref, cov_ref,
                     o_ref, mask_ref):
    qb = pl.program_id(2)
    t0 = qb * QBLK
    q = q_ref[...]
    qs = jnp.concatenate([q[:, j * LANE:(j + 1) * LANE] for j in range(NSA_GROUP)], axis=0).astype(BF16)
    row_t = t0 + lax.broadcasted_iota(jnp.int32, (QBLK, LANE), 0)
    lane_i = lax.broadcasted_iota(jnp.int32, (QBLK, LANE), 1)

    kc = ckv_ref[0, 0, 0].astype(BF16)
    vc = ckv_ref[0, 1, 0].astype(BF16)
    s = lax.dot_general(qs, kc, _NT, preferred_element_type=F32) * ATTN_SCALE
    s = s.reshape(NSA_GROUP, QBLK, LANE) + bc_ref[...]
    valid = (row_t >= lane_i * CMP_STRIDE + (CMP_BLOCK - 1))[None]
    s = jnp.where(valid, s, NEG_INF)
    e = jnp.where(valid, jnp.exp(s - jnp.max(s, axis=-1, keepdims=True)), 0.0)
    den = jnp.sum(e, axis=-1, keepdims=True)
    p = e / jnp.where(den > 0.0, den, 1.0)
    o_cmp = jnp.dot(p.reshape(NSA_GROUP * QBLK, LANE).astype(BF16), vc,
                    preferred_element_type=F32).reshape(NSA_GROUP, QBLK, LANE)

    t_abs = t0 + lax.broadcasted_iota(jnp.int32, (SLC_PAD, LANE), 1)
    sel_t = _select_blocks(p[0] + p[1] + p[2] + p[3], cov_ref[...], t_abs, SEQ // SLC_BLOCK,
                           transpose=False).astype(BF16)
    for kt in range(N_QBLK):
        mask_ref[kt] = jnp.dot(ex_ref[kt], sel_t, preferred_element_type=F32)

    qt = jnp.concatenate([q[:, j * LANE:(j + 1) * LANE].T for j in range(NSA_GROUP)], axis=1).astype(BF16)
    key_i = lax.broadcasted_iota(jnp.int32, (QBLK, LANE), 0)
    qry_t = t0 + lax.broadcasted_iota(jnp.int32, (QBLK, LANE), 1)
    cols = NSA_GROUP * QBLK

    def attend(k_ref, v_ref, lo, use_sel):
        def step(kt, carry):
            m, l, acc = carry
            k0 = pl.multiple_of(kt * QBLK, QBLK)
            k = k_ref[pl.ds(k0, QBLK), :].astype(BF16)
            vt = v_ref[pl.ds(k0, QBLK), :].T.astype(BF16)
            sc = jnp.dot(k, qt, preferred_element_type=F32) * ATTN_SCALE + b3_ref[jnp.minimum(qb - kt, 2), 0]
            dist = qry_t - (k0 + key_i)
            ok = (dist >= 0) & ((mask_ref[kt] > 0.5) if use_sel else (dist < WINDOW))
            sc = jnp.where(jnp.concatenate([ok] * NSA_GROUP, axis=1), sc, NEG_INF)
            m_new = jnp.maximum(m, jnp.max(sc, axis=0, keepdims=True))
            alpha = jnp.exp(m - m_new)
            pr = jnp.exp(sc - m_new)
            l = alpha * l + jnp.sum(pr, axis=0, keepdims=True)
            return m_new, l, alpha * acc + jnp.dot(vt, pr.astype(BF16), preferred_element_type=F32)

        init = (jnp.full((1, cols), NEG_INF, F32), jnp.zeros((1, cols), F32), jnp.zeros((LANE, cols), F32))
        _, l, acc = lax.fori_loop(lo, qb + 1, step, init)
        return acc / l

    o_slc = attend(ks_ref, vs_ref, 0, True)
    o_win = attend(kw_ref, vw_ref, jnp.maximum(qb - WINDOW // QBLK, 0), False)

    gs = jax.nn.sigmoid(gl_ref[...])
    for j in range(NSA_GROUP):
        js = slice(j * LANE, (j + 1) * LANE)
        o_ref[:, js] = (gs[:, j:j + 1] * o_cmp[j] + gs[:, NSA_GROUP + j:NSA_GROUP + j + 1] * o_slc[:, js].T
                        + gs[:, 2 * NSA_GROUP + j:2 * NSA_GROUP + j + 1] * o_win[:, js].T)


def _nsa_prompt(p_p, ckv, near, b_cmp, expand, cov_t):
    kv_blk = COL_KV // LANE

    def kv_spec(branch):
        return pl.BlockSpec((SEQ, LANE), lambda b, g, i: (b, kv_blk + branch * NSA_KV_HEADS + g))

    return pl.pallas_call(
        _nsa_prompt_body,
        out_shape=SDS((N_PROMPT, NSA_WIDTH), F32),
        grid=(BATCH, NSA_KV_HEADS, N_QBLK),
        in_specs=[pl.BlockSpec((QBLK, NSA_GROUP * LANE), lambda b, g, i: (b * N_QBLK + i, g)),
                  pl.BlockSpec((1, 2, 1, N_CMP_PAD, LANE), lambda b, g, i: (b, 0, g, 0, 0)),
                  kv_spec(2), kv_spec(3), kv_spec(4), kv_spec(5),
                  pl.BlockSpec((QBLK, LANE), lambda b, g, i: (b * N_QBLK + i, COL_G // LANE + g)),
                  pl.BlockSpec((NSA_GROUP, QBLK, LANE), lambda b, g, i: (g, i, 0)),
                  pl.BlockSpec((3, 1, QBLK, NSA_GROUP * QBLK), lambda b, g, i: (0, g, 0, 0)),
                  _full_spec(expand.shape), _full_spec(cov_t.shape)],
        out_specs=pl.BlockSpec((QBLK, NSA_GROUP * LANE), lambda b, g, i: (b * N_QBLK + i, g)),
        scratch_shapes=[pltpu.VMEM((N_QBLK, QBLK, LANE), F32)],
        compiler_params=_cparams(("arbitrary", "arbitrary", "arbitrary")),
        name="nsa_prompt",
    )(p_p, ckv, p_p, p_p, p_p, p_p, p_p, b_cmp, near, expand, cov_t)


def _softmax_rows(s):
    e = jnp.exp(s - jnp.max(s, axis=-1, keepdims=True))
    return e / jnp.sum(e, axis=-1, keepdims=True)


def _nsa_sample_body(pt_ref, *refs):
    pages = refs[:N_PAGES]
    (win_ref, q_ref, kvn_ref, gl_ref, w1_ref, pe_ref, b1_ref, w2_ref, b2_ref,
     bc_ref, bs_ref, bw_ref, oks_ref, okw_ref, ex_ref, cov_ref, o_ref) = refs[N_PAGES:]
    rows_per_page = PAGE_SIZE // CMP_STRIDE

    ckv = []
    for s in range(2):
        def load(j, s=s):
            return jnp.concatenate(
                [pages[pg][0, 0, pl.ds(j * PAGE_SLOTS + s * NSA_KV_HEADS + g, rows_per_page,
                                       stride=CMP_STRIDE * PAGE_SLOTS), :]
                 for g in range(NSA_KV_HEADS) for pg in range(N_PAGES)], axis=0)

        ckv.append(_compress_rows(load, s, w1_ref, pe_ref, b1_ref, w2_ref, b2_ref))

    q = q_ref[...]
    kvn = kvn_ref[...]
    gs = jax.nn.sigmoid(gl_ref[...])
    rows = NSA_GROUP * DEC_SEQ
    lane_i = lax.broadcasted_iota(jnp.int32, (rows, LANE), 1)
    t_abs = PAST_LEN + lax.broadcasted_iota(jnp.int32, (SLC_PAD, LANE), 1)
    pad_keys = jnp.zeros((LANE - DEC_SEQ, LANE), F32)
    ok_slc = oks_ref[...] > 0.5
    ok_win = okw_ref[...] > 0.5

    def new_rows(branch, g):
        c0 = (branch * NSA_KV_HEADS + g) * LANE
        return kvn[:, c0:c0 + LANE]

    for g in range(NSA_KV_HEADS):
        qg = jnp.concatenate([q[:, (g * NSA_GROUP + j) * LANE:(g * NSA_GROUP + j + 1) * LANE]
                              for j in range(NSA_GROUP)], axis=0).astype(BF16)

        kc = ckv[0][g * N_CMP_PAD:(g + 1) * N_CMP_PAD].astype(BF16)
        vc = ckv[1][g * N_CMP_PAD:(g + 1) * N_CMP_PAD].astype(BF16)
        s = lax.dot_general(qg, kc, _NT, preferred_element_type=F32) * ATTN_SCALE + bc_ref[g]
        p = _softmax_rows(jnp.where(lane_i < N_CMP_PAD - 1, s, NEG_INF))
        o_cmp = jnp.dot(p.astype(BF16), vc, preferred_element_type=F32)

        psum = p[0:DEC_SEQ]
        for j in range(1, NSA_GROUP):
            psum = psum + p[j * DEC_SEQ:(j + 1) * DEC_SEQ]
        psum = jnp.concatenate([psum, jnp.zeros((LANE - DEC_SEQ, LANE), F32)], axis=0)
        sel = _select_blocks(psum, cov_ref[...], t_abs, S_KEYS // SLC_BLOCK - 1).astype(BF16)
        mask = jnp.dot(sel, ex_ref[...], preferred_element_type=F32)[:DEC_SEQ]
        mask = jnp.concatenate([mask] * NSA_GROUP, axis=0) > 0.5

        def page_rows(pg, slot):
            return pages[pg][0, 0, pl.ds(slot * NSA_KV_HEADS + g, PAGE_SIZE, stride=PAGE_SLOTS), :]

        def win_rows(slot):
            return win_ref[0, 0, pl.ds(slot * NSA_KV_HEADS + g, WINDOW, stride=WIN_SLOTS), :]

        kk = jnp.concatenate([page_rows(pg, 2) for pg in range(N_PAGES)]
                             + [new_rows(2, g), pad_keys], axis=0).astype(BF16)
        vv = jnp.concatenate([page_rows(pg, 3) for pg in range(N_PAGES)]
                             + [new_rows(3, g), pad_keys], axis=0).astype(BF16)
        s = lax.dot_general(qg, kk, _NT, preferred_element_type=F32) * ATTN_SCALE + bs_ref[g]
        p = _softmax_rows(jnp.where(mask & ok_slc, s, NEG_INF))
        o_slc = jnp.dot(p.astype(BF16), vv, preferred_element_type=F32)

        wk = jnp.concatenate([win_rows(0), new_rows(4, g), pad_keys], axis=0).astype(BF16)
        wv = jnp.concatenate([win_rows(1), new_rows(5, g), pad_keys], axis=0).astype(BF16)
        s = lax.dot_general(qg, wk, _NT, preferred_element_type=F32) * ATTN_SCALE + bw_ref[g]
        p = _softmax_rows(jnp.where(ok_win, s, NEG_INF))
        o_win = jnp.dot(p.astype(BF16), wv, preferred_element_type=F32)

        for j in range(NSA_GROUP):
            rs = slice(j * DEC_SEQ, (j + 1) * DEC_SEQ)
            c = g * LANE + j
            h = g * NSA_GROUP + j
            o_ref[:, h * LANE:(h + 1) * LANE] = (
                gs[:, c:c + 1] * o_cmp[rs] + gs[:, c + NSA_GROUP:c + NSA_GROUP + 1] * o_slc[rs]
                + gs[:, c + 2 * NSA_GROUP:c + 2 * NSA_GROUP + 1] * o_win[rs])


def _nsa_sample(page_table, cache, win_state, p_s, cw, tables, cov_t):
    cache = cache.reshape(cache.shape[0], cache.shape[1], PAGE_SIZE * PAGE_SLOTS, LANE)
    win_state = win_state.reshape(win_state.shape[0], win_state.shape[1], WINDOW * WIN_SLOTS, LANE)

    def page_spec(k):
        return pl.BlockSpec((1, 1, PAGE_SIZE * PAGE_SLOTS, LANE), lambda b, pt: (0, pt[b, k], 0, 0))

    consts = list(cw) + list(tables) + [cov_t]
    return pl.pallas_call(
        _nsa_sample_body,
        out_shape=SDS((N_SAMPLE, NSA_WIDTH), F32),
        grid_spec=pltpu.PrefetchScalarGridSpec(
            num_scalar_prefetch=1,
            grid=(DEC_BATCH,),
            in_specs=[page_spec(k) for k in range(N_PAGES)]
            + [pl.BlockSpec((1, 1, WINDOW * WIN_SLOTS, LANE), lambda b, pt: (0, b, 0, 0)),
               pl.BlockSpec((DEC_SEQ, NSA_WIDTH), lambda b, pt: (b, 0)),
               pl.BlockSpec((DEC_SEQ, KV_COLS), lambda b, pt: (b, COL_KV // KV_COLS)),
               pl.BlockSpec((DEC_SEQ, NSA_KV_HEADS * LANE), lambda b, pt: (b, COL_G // (NSA_KV_HEADS * LANE)))]
            + [_full_spec(a.shape) for a in consts],
            out_specs=pl.BlockSpec((DEC_SEQ, NSA_WIDTH), lambda b, pt: (b, 0))),
        compiler_params=_cparams(("arbitrary",)),
        name="nsa_sample",
    )(page_table, *([cache] * N_PAGES), win_state, p_s, p_s, p_s, *consts)


def _gmlp_body(u_ref, v_ref, g_ref, b_ref, w_ref, bs_ref, o_ref, ov_ref):
    for h in range(GM_HEADS):
        sl = slice(h * GM_HEAD_DIM, (h + 1) * GM_HEAD_DIM)
        vn = _normalize(jax.nn.gelu(v_ref[:, sl])) * g_ref[h] + b_ref[h]
        ov_ref[:, sl] = vn
        s = jnp.dot(w_ref[h], vn.astype(BF16), preferred_element_type=F32) + bs_ref[h]
        o_ref[:, sl] = jax.nn.gelu(u_ref[:, sl]) * s


def _gmlp(p, ln_g, ln_b, w_bf16, bs_rows):
    n = p.shape[0]
    row = pl.BlockSpec((GM_CHUNK, GM_WIDTH), lambda i: (i, 0))
    return pl.pallas_call(
        _gmlp_body,
        out_shape=(SDS((n, GM_WIDTH), F32), SDS((n, GM_WIDTH), F32)),
        grid=(n // GM_CHUNK,),
        in_specs=[pl.BlockSpec((GM_CHUNK, GM_WIDTH), lambda i: (i, COL_U // GM_WIDTH)),
                  pl.BlockSpec((GM_CHUNK, GM_WIDTH), lambda i: (i, COL_V // GM_WIDTH)),
                  _full_spec(ln_g.shape), _full_spec(ln_b.shape), _full_spec(w_bf16.shape),
                  _full_spec(bs_rows.shape)],
        out_specs=(row, row),
        compiler_params=_cparams(("arbitrary",)),
        name="gmlp",
    )(p, p, ln_g, ln_b, w_bf16, bs_rows)


def kernel(x_prompt, x_sample, cache_kv_pages, state_kv_window, page_table, c_prompt, c_sample, rel_bias_table,
           w_ada, b_ada, w_in, cmp_pe, cmp_w1, cmp_b1, cmp_w2, cmp_b2, gm_ln_g, gm_ln_b, gm_ws, gm_bs,
           g_norm_nsa, g_norm_gm, w_out, ln1_g, ln1_b, w_router, b_router, w_gu, b_gu, w_down, b_down,
           ln2_g, ln2_b):
    xp = x_prompt.reshape(N_PROMPT, D_MODEL)
    xs = x_sample.reshape(N_SAMPLE, D_MODEL)

    c_all = jnp.concatenate([c_prompt, c_sample, jnp.zeros((4, D_MODEL), F32)], axis=0)
    mod = _ada(c_all, w_ada[0], b_ada[0])
    mod_p = [mod[:BATCH, i * D_MODEL:(i + 1) * D_MODEL].reshape(BATCH, 1, D_MODEL) for i in range(6)]

    def per_token(v, tile):
        return jnp.repeat(v, DEC_SEQ, axis=0).reshape(N_SAMPLE // tile, tile, D_MODEL)

    mod_s = [mod[BATCH:BATCH + DEC_BATCH, i * D_MODEL:(i + 1) * D_MODEL] for i in range(6)]

    w = w_in[0]
    g0 = NSA_WIDTH + KV_COLS
    wg = w[:, g0:g0 + GATE_COLS].reshape(D_MODEL, N_BRANCH, NSA_KV_HEADS, NSA_GROUP)
    wg = jnp.transpose(wg, (0, 2, 1, 3)).reshape(D_MODEL, NSA_KV_HEADS, N_BRANCH * NSA_GROUP)
    wg = jnp.pad(wg, ((0, 0), (0, 0), (0, LANE - N_BRANCH * NSA_GROUP))).reshape(D_MODEL, NSA_KV_HEADS * LANE)
    w_r = jnp.concatenate([w[:, :NSA_WIDTH], w[:, g0 + GATE_COLS:], w[:, NSA_WIDTH:g0], wg], axis=1).astype(BF16)
    p_p = _inproj(xp, mod_p[1], mod_p[0], w_r, SEQ, 512)
    p_s = _inproj(xs, per_token(mod_s[1], 256), per_token(mod_s[0], 256), w_r, None, 256)

    tbl = rel_bias_table[jnp.asarray(_BUCKET_LUT)].T
    cw = (cmp_w1[0].reshape(2, 2, CMP_STRIDE // 2, 2 * HEAD_DIM, HEAD_DIM).astype(BF16), cmp_pe[0],
          cmp_b1[0].reshape(2, 1, HEAD_DIM), cmp_w2[0].astype(BF16), cmp_b2[0].reshape(2, 1, HEAD_DIM))
    ln_g = gm_ln_g[0].reshape(GM_HEADS, 1, GM_HEAD_DIM)
    ln_b = gm_ln_b[0].reshape(GM_HEADS, 1, GM_HEAD_DIM)
    tril = jnp.tril(jnp.ones((GM_CHUNK, GM_CHUNK), bool))

    near, b_cmp, expand = _prompt_tables(tbl)
    ckv_p = _compress_prompt(p_p, cw)
    o_nsa_p = _nsa_prompt(p_p, ckv_p, near, b_cmp, expand, _cover_t(SEQ // SLC_BLOCK))
    w_sp = jnp.where(tril, gm_ws[0], 0).astype(BF16)
    bs_p = jnp.broadcast_to(gm_bs[0][:, :, None], (GM_HEADS, GM_CHUNK, GM_HEAD_DIM))
    gm_p, _ = _gmlp(p_p, ln_g, ln_b, w_sp, bs_p)
    p3 = p_p.reshape(BATCH, SEQ, IN_COLS_PAD)
    half = KV_COLS // N_BRANCH * 2
    prompt_kv_rows = p3[:, :, COL_KV:COL_KV + half].reshape(1, BATCH, SEQ, 4, NSA_KV_HEADS, HEAD_DIM)
    prompt_kv_window = p3[:, SEQ - WINDOW:, COL_KV + half:COL_G].reshape(1, BATCH, WINDOW, 2, NSA_KV_HEADS, HEAD_DIM)

    o_nsa_s = _nsa_sample(page_table, cache_kv_pages, state_kv_window, p_s, cw, _sample_tables(tbl),
                          _cover_t(S_KEYS // SLC_BLOCK - 1))
    seqs = GM_CHUNK // DEC_SEQ
    w_small = jnp.where(tril[:DEC_SEQ, :DEC_SEQ], gm_ws[0][:, :DEC_SEQ, :DEC_SEQ], 0)
    w_ss = jnp.einsum('ab,hij->haibj', jnp.eye(seqs, dtype=F32), w_small).reshape(GM_HEADS, GM_CHUNK, GM_CHUNK)
    bs_s = jnp.broadcast_to(jnp.tile(gm_bs[0][:, :DEC_SEQ], (1, seqs))[:, :, None],
                            (GM_HEADS, GM_CHUNK, GM_HEAD_DIM))
    gm_s, gmv_s = _gmlp(p_s, ln_g, ln_b, w_ss.astype(BF16), bs_s)
    s3 = p_s.reshape(DEC_BATCH, DEC_SEQ, IN_COLS_PAD)
    sample_kv_rows = s3[:, :, COL_KV:COL_KV + half].reshape(1, DEC_BATCH, DEC_SEQ, 4, NSA_KV_HEADS, HEAD_DIM)
    win_new = s3[:, :, COL_KV + half:COL_G].reshape(DEC_BATCH, DEC_SEQ, 2, NSA_KV_HEADS, HEAD_DIM)
    sample_kv_window = jnp.concatenate([state_kv_window[0][:, DEC_SEQ:], win_new], axis=1)[None]
    sample_gmlp_v = gmv_s.reshape(1, DEC_BATCH, DEC_SEQ, GM_HEADS, GM_HEAD_DIM)

    w_ob = w_out[0].astype(BF16)
    mix_p = _matmul(_merge(o_nsa_p, gm_p, g_norm_nsa[0], g_norm_gm[0]), w_ob)
    mix_s = _matmul(_merge(o_nsa_s, gm_s, g_norm_nsa[0], g_norm_gm[0]), w_ob)
    wr = jnp.pad(w_router[0], ((0, 0), (0, 128 - N_EXPERTS))).astype(BF16)
    br = jnp.pad(b_router[0], (0, 128 - N_EXPERTS)).reshape(1, 128)
    y1_p, h_p, ti_p, ga_p = _post1(xp, mix_p, mod_p[2], mod_p[4], mod_p[3], ln1_g[0], ln1_b[0], wr, br, SEQ)
    y1_s, h_s, ti_s, ga_s = _post1(xs, mix_s, per_token(mod_s[2], POST_T), per_token(mod_s[4], POST_T),
                                   per_token(mod_s[3], POST_T), ln1_g[0], ln1_b[0], wr, br, None)

    h = jnp.concatenate([h_p, h_s], axis=0)
    gates = jnp.concatenate([ga_p, ga_s], axis=0)[:, :TOP_K]
    row_tok, pos, route = _route(jnp.concatenate([ti_p, ti_s], axis=0)[:, :TOP_K])
    xg = _moe_gather(row_tok, route[1], h)
    hact = _moe_gate_up(route, xg, w_gu[0], b_gu[0])
    yb = _moe_down(route, hact, w_down[0], b_down[0])
    y_p = _moe_combine(pos[:N_PROMPT * TOP_K], yb, y1_p, gates[:N_PROMPT], mod_p[5], ln2_g[0], ln2_b[0], SEQ, 0)
    y_s = _moe_combine(pos[N_PROMPT * TOP_K:], yb, y1_s, gates[N_PROMPT:], per_token(mod_s[5], COMB_T),
                       ln2_g[0], ln2_b[0], None, 0)
    return (y_p.reshape(BATCH, SEQ, D_MODEL), y_s.reshape(DEC_BATCH, DEC_SEQ, D_MODEL),
            prompt_kv_rows, prompt_kv_window, sample_kv_rows, sample_kv_window, sample_gmlp_v)
```
